```python
import math
import jax, jax.numpy as jnp
from jax import lax
import numpy as np

D_MODEL = 2048
BATCH = 1
SEQ = 16384
DEPTH = 2

PLE_DIM = 256
NORM_EPS = 1e-6
GDN_HEADS = 8
GDN_DK = 128
GDN_DV = 128
GDN_CONV = 4
GDN_CHUNK = 64
SGU_GROUPS = 8
SGU_DIM = 128
SGU_CHUNK = 128
DIFF_HEADS = 8
DIFF_DH = 128
ROPE_THETA = 10000.0
Q_BLOCK = 128
N_EXPERTS = 32
TOP_K = 4
D_FF = 2048
SWIGLU_ALPHA = 1.702
SWIGLU_LIMIT = 7.0

A_QK = GDN_HEADS * GDN_DK
A_V = GDN_HEADS * GDN_DV
A_QKV = 2 * A_QK + A_V
B_W = SGU_GROUPS * SGU_DIM
EVEN_SPLITS = (A_QKV, A_QKV + A_V, A_QKV + A_V + GDN_HEADS,
               A_QKV + A_V + 2 * GDN_HEADS, A_QKV + A_V + 2 * GDN_HEADS + B_W)
EVEN_IN = A_QKV + A_V + 2 * GDN_HEADS + 2 * B_W
EVEN_MIX = A_V + B_W
C_QK = DIFF_HEADS * 2 * DIFF_DH
ODD_IN = 3 * C_QK
ODD_MIX = DIFF_HEADS * 2 * DIFF_DH

kernel_name = "hybrid_gdn_sgu_diffattn_moe_ple"


def rms_norm(x, g, eps=NORM_EPS):
    xf = x.astype(jnp.float32)
    y = xf * lax.rsqrt(jnp.mean(xf * xf, axis=-1, keepdims=True) + eps)
    return (y * g.astype(jnp.float32)).astype(x.dtype)


def layer_norm(x, g, b, eps=1e-5):
    xf = x.astype(jnp.float32)
    mu = jnp.mean(xf, axis=-1, keepdims=True)
    xc = xf - mu
    var = jnp.mean(xc * xc, axis=-1, keepdims=True)
    return (xc * lax.rsqrt(var + eps) * g.astype(jnp.float32) + b.astype(jnp.float32)).astype(x.dtype)


def l2norm(x, eps=1e-6):
    return x * lax.rsqrt(jnp.sum(x * x, axis=-1, keepdims=True) + eps)


def causal_depthwise_conv(x, w):
    k, c = w.shape
    return lax.conv_general_dilated(x, w[:, None, :].astype(x.dtype), window_strides=(1,),
                                    padding=[(k - 1, 0)],
                                    dimension_numbers=("NHC", "HIO", "NHC"),
                                    feature_group_count=c)


def chunk_gated_delta_rule(q, k, v, g, beta):
    bsz, s, h, dk = q.shape
    dv = v.shape[-1]
    c = GDN_CHUNK
    n = s // c

    def chunks(t):
        return jnp.moveaxis(t.reshape((bsz, n, c, h) + t.shape[3:]), 3, 1)

    q, k, v, g, beta = chunks(q), chunks(k), chunks(v), chunks(g), chunks(beta)
    g = jnp.cumsum(g, axis=-1)
    causal = jnp.tril(jnp.ones((c, c), dtype=bool))
    strict = jnp.tril(jnp.ones((c, c), dtype=bool), -1)
    gamma = jnp.exp(jnp.where(causal, g[..., :, None] - g[..., None, :], -jnp.inf))
    k_beta = k * beta[..., None]
    lower = jnp.where(strict, jnp.einsum("bhncd,bhnsd->bhncs", k_beta, k) * gamma, 0.0)
    rhs = jnp.concatenate([v * beta[..., None], k_beta * jnp.exp(g)[..., None]], axis=-1)
    sol = lax.linalg.triangular_solve(jnp.eye(c, dtype=jnp.float32) + lower, rhs,
                                      left_side=True, lower=True)
    u, w = sol[..., :dv], sol[..., dv:]
    a_qk = jnp.einsum("bhncd,bhnsd->bhncs", q, k) * gamma
    g_last = g[..., -1]
    q_dec = q * jnp.exp(g)[..., None]
    k_dec = k * jnp.exp(g_last[..., None] - g)[..., None]
    xs = tuple(jnp.moveaxis(t, 2, 0) for t in (q_dec, k_dec, u, w, a_qk, jnp.exp(g_last)))

    def step(state, inp):
        qd, kd, uc, wc, aqk, dlast = inp
        v_new = uc - jnp.einsum("bhcd,bhde->bhce", wc, state)
        o = jnp.einsum("bhcd,bhde->bhce", qd, state) + jnp.einsum("bhcs,bhse->bhce", aqk, v_new)
        state = state * dlast[..., None, None] + jnp.einsum("bhcd,bhce->bhde", kd, v_new)
        return state, o

    _, o = lax.scan(step, jnp.zeros((bsz, h, dk, dv), jnp.float32), xs)
    return jnp.transpose(o, (1, 0, 3, 2, 4)).reshape(bsz, s, h, dv)


def even_mixer(h, w_in, conv_w, a_log, dt_bias, out_norm, sgu_ln_g, sgu_ln_b, sgu_w, sgu_b, w_out):
    bsz, s, _ = h.shape
    f32 = jnp.float32
    qkv, z, a, b, gu, gv = jnp.split(h @ w_in, list(EVEN_SPLITS), axis=-1)
    qkv = jax.nn.silu(causal_depthwise_conv(qkv, conv_w))
    q, k, v = jnp.split(qkv, [A_QK, 2 * A_QK], axis=-1)
    q = l2norm(q.reshape(bsz, s, GDN_HEADS, GDN_DK).astype(f32)) * (GDN_DK ** -0.5)
    k = l2norm(k.reshape(bsz, s, GDN_HEADS, GDN_DK).astype(f32))
    v = v.reshape(bsz, s, GDN_HEADS, GDN_DV).astype(f32)
    beta = jax.nn.sigmoid(b.astype(f32))
    g = -jnp.exp(a_log.astype(f32)) * jax.nn.softplus(a.astype(f32) + dt_bias.astype(f32))
    o = chunk_gated_delta_rule(q, k, v, g, beta)
    o = rms_norm(o, out_norm) * jax.nn.silu(z.reshape(bsz, s, GDN_HEADS, GDN_DV).astype(f32))
    out_a = o.reshape(bsz, s, A_V).astype(h.dtype)
    n = s // SGU_CHUNK
    gu = jax.nn.gelu(gu, approximate=False).reshape(bsz, n, SGU_CHUNK, SGU_GROUPS, SGU_DIM)
    gv = jax.nn.gelu(gv, approximate=False).reshape(bsz, n, SGU_CHUNK, SGU_GROUPS, SGU_DIM)
    gv = layer_norm(gv, sgu_ln_g, sgu_ln_b)
    w_sp = jnp.where(jnp.tril(jnp.ones((SGU_CHUNK, SGU_CHUNK), dtype=bool)), sgu_w, 0.0)
    mixed = jnp.einsum("gts,bnsgc->bntgc", w_sp, gv) + sgu_b.T[:, :, None]
    out_b = (gu * mixed).reshape(bsz, s, B_W)
    return jnp.concatenate([out_a, out_b], axis=-1) @ w_out


def apply_rope(x, positions):
    half = x.shape[-1] // 2
    inv_freq = ROPE_THETA ** (-jnp.arange(half, dtype=jnp.float32) / half)
    ang = (positions.astype(jnp.float32)[..., None] * inv_freq)[:, :, None, None, :]
    cos, sin = jnp.cos(ang), jnp.sin(ang)
    x1, x2 = x[..., :half].astype(jnp.float32), x[..., half:].astype(jnp.float32)
    return jnp.concatenate([x1 * cos - x2 * sin, x2 * cos + x1 * sin], axis=-1).astype(x.dtype)


def causal_diff_attention(q, k, v, lam):
    bsz, s, h, _, dh = q.shape
    nb = s // Q_BLOCK
    qb = jnp.moveaxis(q.reshape(bsz, nb, Q_BLOCK, h, 2, dh), 1, 0)
    kf, vf = k.astype(jnp.float32), v.astype(jnp.float32)
    key_pos = jnp.arange(s)

    def block(args):
        q_blk, blk = args
        scores = jnp.einsum("bqhmd,bkhmd->bhmqk", q_blk.astype(jnp.float32), kf) * (dh ** -0.5)
        q_pos = blk * Q_BLOCK + jnp.arange(Q_BLOCK)
        scores = jnp.where(key_pos[None, :] <= q_pos[:, None], scores, -jnp.inf)
        probs = jax.nn.softmax(scores, axis=-1)
        diff = probs[:, :, 0] - lam * probs[:, :, 1]
        return jnp.einsum("bhqk,bkhe->bqhe", diff, vf)

    o = lax.map(block, (qb, jnp.arange(nb)))
    return jnp.moveaxis(o, 0, 1).reshape(bsz, s, h, 2 * dh).astype(v.dtype)


def odd_mixer(h, positions, layer_idx, w_in, lam, subln, w_out):
    bsz, s, _ = h.shape
    q, k, v = jnp.split(h @ w_in, [C_QK, 2 * C_QK], axis=-1)
    q = apply_rope(q.reshape(bsz, s, DIFF_HEADS, 2, DIFF_DH), positions)
    k = apply_rope(k.reshape(bsz, s, DIFF_HEADS, 2, DIFF_DH), positions)
    v = v.reshape(bsz, s, DIFF_HEADS, 2 * DIFF_DH)
    lam_init = 0.8 - 0.6 * math.exp(-0.3 * layer_idx)
    lf = lam.astype(jnp.float32)
    lam_full = jnp.exp(jnp.dot(lf[0], lf[1])) - jnp.exp(jnp.dot(lf[2], lf[3])) + lam_init
    o = causal_diff_attention(q, k, v, lam_full)
    o = rms_norm(o, subln) * (1.0 - lam_init)
    return o.reshape(bsz, s, ODD_MIX) @ w_out


def moe(h, router_w, router_b, w1, b1, w2, b2):
    bsz, s, d = h.shape
    xt = h.reshape(bsz * s, d)
    logits = (xt @ router_w + router_b).astype(jnp.float32)
    top_vals, top_idx = lax.top_k(logits, TOP_K)
    gates = jax.nn.softmax(top_vals, axis=-1)
    combine = jnp.sum(jax.nn.one_hot(top_idx, N_EXPERTS, dtype=jnp.float32) * gates[..., None], axis=1)
    combine = combine.astype(xt.dtype)
    out = jnp.zeros_like(xt)
    for e in range(N_EXPERTS):
        x_glu, x_lin = jnp.split(xt @ w1[e] + b1[e], 2, axis=-1)
        x_glu = jnp.minimum(x_glu, SWIGLU_LIMIT)
        x_lin = jnp.clip(x_lin, -SWIGLU_LIMIT, SWIGLU_LIMIT)
        act = x_glu * jax.nn.sigmoid(SWIGLU_ALPHA * x_glu) * (x_lin + 1.0)
        out = out + combine[:, e:e + 1] * (act @ w2[e] + b2[e])
    return out.reshape(bsz, s, d)


def setup_inputs(seed: int = 0) -> dict:
    key = jax.random.key(seed)
    ks = iter(jax.random.split(key, 40))
    f32 = jnp.float32

    def nrm(shape, std):
        return jax.random.normal(next(ks), shape, f32) * std

    def gain(shape):
        return 1.0 + nrm(shape, 0.02)

    ne, no = (DEPTH + 1) // 2, DEPTH // 2
    x = nrm((BATCH, SEQ, D_MODEL), 1.0)
    p = nrm((DEPTH, BATCH, SEQ, PLE_DIM), 1.0)
    positions = jnp.broadcast_to(jnp.arange(SEQ, dtype=jnp.int32), (BATCH, SEQ))
    mix_norm = gain((DEPTH, D_MODEL))
    ffn_norm = gain((DEPTH, D_MODEL))
    ev_w_in = nrm((ne, D_MODEL, EVEN_IN), D_MODEL ** -0.5)
    ev_conv_w = nrm((ne, GDN_CONV, A_QKV), GDN_CONV ** -0.5)
    ev_a_log = jnp.log(jax.random.uniform(next(ks), (ne, GDN_HEADS), f32, 1.0, 16.0))
    dt = jnp.exp(jax.random.uniform(next(ks), (ne, GDN_HEADS), f32, math.log(1e-3), math.log(1e-1)))
    ev_dt_bias = dt + jnp.log(-jnp.expm1(-dt))
    ev_out_norm = gain((ne, GDN_DV))
    ev_sgu_ln_g = gain((ne, SGU_GROUPS, SGU_DIM))
    ev_sgu_ln_b = nrm((ne, SGU_GROUPS, SGU_DIM), 0.02)
    ev_sgu_w = nrm((ne, SGU_GROUPS, SGU_CHUNK, SGU_CHUNK), 0.5 * SGU_CHUNK ** -0.5)
    ev_sgu_b = 1.0 + nrm((ne, SGU_GROUPS, SGU_CHUNK), 0.1)
    ev_w_out = nrm((ne, EVEN_MIX, D_MODEL), EVEN_MIX ** -0.5)
    od_w_in = nrm((no, D_MODEL, ODD_IN), D_MODEL ** -0.5)
    od_lambda = nrm((no, 4, DIFF_DH), 0.1)
    od_subln = gain((no, 2 * DIFF_DH))
    od_w_out = nrm((no, ODD_MIX, D_MODEL), ODD_MIX ** -0.5)
    router_w = nrm((DEPTH, D_MODEL, N_EXPERTS), D_MODEL ** -0.5)
    router_b = nrm((DEPTH, N_EXPERTS), 0.01)
    exp_w1 = nrm((DEPTH, N_EXPERTS, D_MODEL, 2 * D_FF), D_MODEL ** -0.5)
    exp_b1 = nrm((DEPTH, N_EXPERTS, 2 * D_FF), 0.01)
    exp_w2 = nrm((DEPTH, N_EXPERTS, D_FF, D_MODEL), D_FF ** -0.5)
    exp_b2 = nrm((DEPTH, N_EXPERTS, D_MODEL), 0.01)
    ple_proj = nrm((DEPTH, PLE_DIM, D_MODEL), PLE_DIM ** -0.5)
    ple_norm = gain((DEPTH, D_MODEL))
    ple_gate = nrm((DEPTH, D_MODEL, D_MODEL), D_MODEL ** -0.5)
    final_norm = gain((D_MODEL,))
    return {"x": x, "p": p, "positions": positions, "mix_norm": mix_norm, "ffn_norm": ffn_norm,
            "ev_w_in": ev_w_in, "ev_conv_w": ev_conv_w, "ev_a_log": ev_a_log, "ev_dt_bias": ev_dt_bias,
            "ev_out_norm": ev_out_norm, "ev_sgu_ln_g": ev_sgu_ln_g, "ev_sgu_ln_b": ev_sgu_ln_b,
            "ev_sgu_w": ev_sgu_w, "ev_sgu_b": ev_sgu_b, "ev_w_out": ev_w_out,
            "od_w_in": od_w_in, "od_lambda": od_lambda, "od_subln": od_subln, "od_w_out": od_w_out,
            "router_w": router_w, "router_b": router_b, "exp_w1": exp_w1, "exp_b1": exp_b1,
            "exp_w2": exp_w2, "exp_b2": exp_b2, "ple_proj": ple_proj, "ple_norm": ple_norm,
            "ple_gate": ple_gate, "final_norm": final_norm}


def reference(x, p, positions, mix_norm, ffn_norm, ev_w_in, ev_conv_w, ev_a_log, ev_dt_bias,
              ev_out_norm, ev_sgu_ln_g, ev_sgu_ln_b, ev_sgu_w, ev_sgu_b, ev_w_out,
              od_w_in, od_lambda, od_subln, od_w_out, router_w, router_b, exp_w1, exp_b1,
              exp_w2, exp_b2, ple_proj, ple_norm, ple_gate, final_norm):
    h = x
    for i in range(DEPTH):
        hn = rms_norm(h, mix_norm[i])
        j = i // 2
        if i % 2 == 0:
            mix = even_mixer(hn, ev_w_in[j], ev_conv_w[j], ev_a_log[j], ev_dt_bias[j], ev_out_norm[j],
                             ev_sgu_ln_g[j], ev_sgu_ln_b[j], ev_sgu_w[j], ev_sgu_b[j], ev_w_out[j])
        else:
            mix = odd_mixer(hn, positions, i, od_w_in[j], od_lambda[j], od_subln[j], od_w_out[j])
        h = h + mix
        h = h + moe(rms_norm(h, ffn_norm[i]), router_w[i], router_b[i],
                    exp_w1[i], exp_b1[i], exp_w2[i], exp_b2[i])
        gate = jax.nn.sigmoid(rms_norm(h, ple_norm[i]) @ ple_gate[i])
        h = h + (p[i] @ ple_proj[i]) * gate
    return rms_norm(h, final_norm)
```

```python
import functools
import math

import jax
import jax.numpy as jnp
from jax import lax
from jax.experimental import pallas as pl
from jax.experimental.pallas import tpu as pltpu

F32 = jnp.float32
BF16 = jnp.bfloat16
HI = lax.Precision.HIGHEST

NORM_EPS = 1e-6
L2_EPS = 1e-6
LN_EPS = 1e-5
LANES = 128
SUBLANES = 8
VMEM_LIMIT = 56 * 1024 * 1024

GDN_HEADS = 8
GDN_DK = 128
GDN_DV = 128
GDN_CONV = 4
GDN_CHUNK = 128
SGU_GROUPS = 8
SGU_DIM = 128
SGU_CHUNK = 128
DIFF_HEADS = 8
DIFF_DH = 128
ROPE_THETA = 10000.0
TOP_K = 4
SWIGLU_ALPHA = 1.702
SWIGLU_LIMIT = 7.0

A_QK = GDN_HEADS * GDN_DK
A_V = GDN_HEADS * GDN_DV
A_QKV = 2 * A_QK + A_V
B_W = SGU_GROUPS * SGU_DIM
C_QK = DIFF_HEADS * 2 * DIFF_DH


def _params(n_axes):
    return pltpu.CompilerParams(dimension_semantics=("arbitrary",) * n_axes,
                                vmem_limit_bytes=VMEM_LIMIT)


def _rms(x, g):
    return x * lax.rsqrt(jnp.mean(x * x, axis=-1, keepdims=True) + NORM_EPS) * g


def _gelu(x):
    return 0.5 * x * (1.0 + lax.erf(x * (2.0 ** -0.5)))


def _softplus(x):
    return jnp.maximum(x, 0.0) + jnp.log1p(jnp.exp(-jnp.abs(x)))


def _norm_matmul_kernel(x_ref, g_ref, w_ref, o_ref, xn_ref):
    @pl.when(pl.program_id(1) == 0)
    def _():
        xn_ref[...] = _rms(x_ref[...], g_ref[...]).astype(BF16)

    o_ref[...] = jnp.dot(xn_ref[...], w_ref[...], preferred_element_type=F32).astype(o_ref.dtype)


def _norm_matmul_rope_kernel(x_ref, g_ref, w_ref, cos_ref, sin_ref, o_ref, xn_ref, *, n_q_blocks,
                             n_rope_blocks, q_scale):
    j = pl.program_id(1)

    @pl.when(j == 0)
    def _():
        xn_ref[...] = _rms(x_ref[...], g_ref[...]).astype(BF16)

    y = jnp.dot(xn_ref[...], w_ref[...], preferred_element_type=F32)

    @pl.when(j < n_rope_blocks)
    def _():
        cos = cos_ref[...]
        sin = sin_ref[...]
        scale = jnp.where(j < n_q_blocks, q_scale, 1.0).astype(F32)
        for c in range(y.shape[1] // LANES):
            yc = y[:, c * LANES:(c + 1) * LANES]
            rot = pltpu.roll(yc, LANES // 2, axis=1)
            o_ref[:, c * LANES:(c + 1) * LANES] = ((yc * cos + rot * sin) * scale).astype(o_ref.dtype)

    @pl.when(j >= n_rope_blocks)
    def _():
        o_ref[...] = y.astype(o_ref.dtype)


def _norm_matmul(x, gain, w, *, tm, tn, out_dtype, rope=None):
    t, d = x.shape
    n = w.shape[1]
    grid = (t // tm, n // tn)
    in_specs = [pl.BlockSpec((tm, d), lambda i, j: (i, 0)),
                pl.BlockSpec((1, d), lambda i, j: (0, 0)),
                pl.BlockSpec((d, tn), lambda i, j: (0, j))]
    args = [x, gain.reshape(1, d), w]
    if rope is None:
        body = _norm_matmul_kernel
    else:
        cos, sin, n_q_cols, n_rope_cols, q_scale = rope
        body = functools.partial(_norm_matmul_rope_kernel, n_q_blocks=n_q_cols // tn,
                                 n_rope_blocks=n_rope_cols // tn, q_scale=q_scale)
        in_specs += [pl.BlockSpec((tm, LANES), lambda i, j: (i, 0)),
                     pl.BlockSpec((tm, LANES), lambda i, j: (i, 0))]
        args += [cos, sin]
    return pl.pallas_call(
        body,
        grid=grid,
        in_specs=in_specs,
        out_specs=pl.BlockSpec((tm, tn), lambda i, j: (i, j)),
        out_shape=jax.ShapeDtypeStruct((t, n), out_dtype),
        scratch_shapes=[pltpu.VMEM((tm, d), BF16)],
        compiler_params=_params(2),
    )(*args)


def _matmul_residual_kernel(*refs, n_acts):
    a_refs = refs[:n_acts]
    w_refs = refs[n_acts:2 * n_acts]
    res_ref = refs[2 * n_acts]
    o_ref = refs[2 * n_acts + 1]
    acc = res_ref[...]
    for a_ref, w_ref in zip(a_refs, w_refs):
        acc = acc + jnp.dot(a_ref[...], w_ref[...], preferred_element_type=F32)
    o_ref[...] = acc


def _matmul_residual(acts, ws, res, *, tm, tn):
    t, n = res.shape
    n_acts = len(acts)
    in_specs = [pl.BlockSpec((tm, a.shape[1]), lambda i, j: (i, 0)) for a in acts]
    in_specs += [pl.BlockSpec((w.shape[0], tn), lambda i, j: (0, j)) for w in ws]
    in_specs += [pl.BlockSpec((tm, tn), lambda i, j: (i, j))]
    return pl.pallas_call(
        functools.partial(_matmul_residual_kernel, n_acts=n_acts),
        grid=(t // tm, n // tn),
        in_specs=in_specs,
        out_specs=pl.BlockSpec((tm, tn), lambda i, j: (i, j)),
        out_shape=jax.ShapeDtypeStruct((t, n), F32),
        compiler_params=_params(2),
    )(*acts, *ws, res)


def _gdn_kernel(q_ref, k_ref, v_ref, z_ref, ab_ref, cwq_ref, cwk_ref, cwv_ref, alog_ref, dtb_ref,
                onorm_ref, o_ref, buf_ref, s_ref, *, tb):
    c_len = GDN_CHUNK
    h = pl.program_id(0)
    halo = SUBLANES

    @pl.when(pl.program_id(1) == 0)
    def _():
        buf_ref[:, 0:halo, :] = jnp.zeros((3, halo, LANES), F32)
        s_ref[...] = jnp.zeros_like(s_ref)

    def conv_silu(i, x_ref, cw_ref):
        x = x_ref[...]
        buf_ref[i, halo:halo + tb, :] = x
        cw = cw_ref[...]
        acc = x * cw[GDN_CONV - 1:GDN_CONV, :]
        for s in range(1, GDN_CONV):
            acc = acc + buf_ref[i, halo - s:halo - s + tb, :] * cw[GDN_CONV - 1 - s:GDN_CONV - s, :]
        buf_ref[i, 0:halo, :] = x[tb - halo:tb, :]
        return acc * jax.nn.sigmoid(acc)

    q = conv_silu(0, q_ref, cwq_ref)
    k = conv_silu(1, k_ref, cwk_ref)
    v = conv_silu(2, v_ref, cwv_ref)
    q = q * lax.rsqrt(jnp.sum(q * q, axis=-1, keepdims=True) + L2_EPS) * (GDN_DK ** -0.5)
    k = k * lax.rsqrt(jnp.sum(k * k, axis=-1, keepdims=True) + L2_EPS)

    ab = ab_ref[...]
    lane = lax.broadcasted_iota(jnp.int32, (1, LANES), 1)
    g_all = -jnp.exp(alog_ref[...]) * _softplus(ab + dtb_ref[...])
    g_col = jnp.sum(jnp.where(lane == h, g_all, 0.0), axis=1, keepdims=True)
    beta = jnp.sum(jnp.where(lane == GDN_HEADS + h, jax.nn.sigmoid(ab), 0.0), axis=1, keepdims=True)

    row = lax.broadcasted_iota(jnp.int32, (c_len, c_len), 0)
    col = lax.broadcasted_iota(jnp.int32, (c_len, c_len), 1)
    tri_incl = (row >= col).astype(F32)
    eye = (row == col).astype(F32)
    onorm = onorm_ref[...]
    nt = (((1,), (1,)), ((), ()))

    for c in range(tb // c_len):
        sl = slice(c * c_len, (c + 1) * c_len)
        qc, kc, vc, bc = q[sl], k[sl], v[sl], beta[sl]
        gb = jnp.dot(tri_incl, jnp.broadcast_to(g_col[sl], (c_len, LANES)), precision=HI,
                     preferred_element_type=F32)
        dif = gb - gb.T
        gam = jnp.where(row >= col, jnp.exp(jnp.minimum(dif, 0.0)), 0.0)
        kb = kc * bc
        kk = lax.dot_general(kb, kc, nt, precision=HI, preferred_element_type=F32)
        low = jnp.where(row > col, kk * gam, 0.0)
        inv = eye - low
        pw = low
        for _ in range(int(math.log2(c_len)) - 1):
            pw = jnp.dot(pw, pw, precision=HI, preferred_element_type=F32)
            inv = inv + jnp.dot(inv, pw, precision=HI, preferred_element_type=F32)
        eg = jnp.exp(gb)
        u = jnp.dot(inv, vc * bc, precision=HI, preferred_element_type=F32)
        w = jnp.dot(inv, kb * eg, precision=HI, preferred_element_type=F32)
        aqk = lax.dot_general(qc, kc, nt, precision=HI, preferred_element_type=F32) * gam
        g_last = gb[c_len - 1:c_len, :]
        qd = qc * eg
        kd = kc * jnp.exp(g_last - gb)
        state = s_ref[...]
        v_new = u - jnp.dot(w, state, precision=HI, preferred_element_type=F32)
        o = (jnp.dot(qd, state, precision=HI, preferred_element_type=F32)
             + jnp.dot(aqk, v_new, precision=HI, preferred_element_type=F32))
        s_ref[...] = state * jnp.exp(g_last) + jnp.dot(kd.T, v_new, precision=HI,
                                                        preferred_element_type=F32)
        zc = z_ref[sl, :]
        o = _rms(o, onorm) * (zc * jax.nn.sigmoid(zc))
        o_ref[sl, :] = o.astype(o_ref.dtype)


def _gdn(proj, conv_w, a_log, dt_bias, out_norm, *, tb, ab_block):
    t = proj.shape[0]
    hds = GDN_HEADS

    def col(off):
        return pl.BlockSpec((tb, LANES), lambda h, b, off=off: (b, off + h))

    def cw(off):
        return pl.BlockSpec((GDN_CONV, LANES), lambda h, b, off=off: (0, off + h))

    vec = pl.BlockSpec((1, LANES), lambda h, b: (0, 0))
    pad = LANES - hds
    return pl.pallas_call(
        functools.partial(_gdn_kernel, tb=tb),
        grid=(hds, t // tb),
        in_specs=[col(0), col(hds), col(2 * hds), col(3 * hds),
                  pl.BlockSpec((tb, LANES), lambda h, b: (b, ab_block)),
                  cw(0), cw(hds), cw(2 * hds), vec, vec, vec],
        out_specs=pl.BlockSpec((tb, LANES), lambda h, b: (b, h)),
        out_shape=jax.ShapeDtypeStruct((t, A_V), BF16),
        scratch_shapes=[pltpu.VMEM((3, SUBLANES + tb, LANES), F32),
                        pltpu.VMEM((GDN_DK, GDN_DV), F32)],
        compiler_params=_params(2),
    )(proj, proj, proj, proj, proj, conv_w, conv_w, conv_w,
      jnp.pad(a_log, (0, pad)).reshape(1, LANES), jnp.pad(dt_bias, (0, pad)).reshape(1, LANES),
      out_norm.reshape(1, LANES))


def _sgu_kernel(gu_ref, gv_ref, lng_ref, lnb_ref, w_ref, b_ref, o_ref, *, tb):
    c_len = SGU_CHUNK
    row = lax.broadcasted_iota(jnp.int32, (c_len, c_len), 0)
    col = lax.broadcasted_iota(jnp.int32, (c_len, c_len), 1)
    w = jnp.where(row >= col, w_ref[...], 0.0)
    bias = b_ref[...]
    gu = _gelu(gu_ref[...])
    gv = _gelu(gv_ref[...])
    mu = jnp.mean(gv, axis=-1, keepdims=True)
    xc = gv - mu
    var = jnp.mean(xc * xc, axis=-1, keepdims=True)
    gv = xc * lax.rsqrt(var + LN_EPS) * lng_ref[...] + lnb_ref[...]
    for c in range(tb // c_len):
        sl = slice(c * c_len, (c + 1) * c_len)
        mixed = jnp.dot(w, gv[sl], precision=HI, preferred_element_type=F32) + bias
        o_ref[sl, :] = (gu[sl] * mixed).astype(o_ref.dtype)


def _sgu(proj, ln_g, ln_b, sgu_w, sgu_b, *, tb, gu_block, gv_block):
    t = proj.shape[0]
    g = SGU_GROUPS
    return pl.pallas_call(
        functools.partial(_sgu_kernel, tb=tb),
        grid=(t // tb, g),
        in_specs=[pl.BlockSpec((tb, LANES), lambda b, gi: (b, gu_block + gi)),
                  pl.BlockSpec((tb, LANES), lambda b, gi: (b, gv_block + gi)),
                  pl.BlockSpec((None, 1, SGU_DIM), lambda b, gi: (gi, 0, 0)),
                  pl.BlockSpec((None, 1, SGU_DIM), lambda b, gi: (gi, 0, 0)),
                  pl.BlockSpec((None, SGU_CHUNK, SGU_CHUNK), lambda b, gi: (gi, 0, 0)),
                  pl.BlockSpec((None, SGU_CHUNK, 1), lambda b, gi: (gi, 0, 0))],
        out_specs=pl.BlockSpec((tb, LANES), lambda b, gi: (b, gi)),
        out_shape=jax.ShapeDtypeStruct((t, B_W), BF16),
        compiler_params=_params(2),
    )(proj, proj, ln_g.reshape(g, 1, SGU_DIM), ln_b.reshape(g, 1, SGU_DIM), sgu_w,
      sgu_b.reshape(g, SGU_CHUNK, 1))


def _rope_table_kernel(pos_ref, freq_ref, cos_ref, sin_ref):
    ang = pos_ref[...].astype(F32) * freq_ref[...]
    lane = lax.broadcasted_iota(jnp.int32, ang.shape, 1)
    cos_ref[...] = jnp.cos(ang)
    sin_ref[...] = jnp.where(lane < LANES // 2, -1.0, 1.0) * jnp.sin(ang)


def _rope_tables(positions, *, tm):
    t = positions.shape[0]
    half = DIFF_DH // 2
    inv_freq = ROPE_THETA ** (-jnp.arange(half, dtype=F32) / half)
    freq = jnp.concatenate([inv_freq, inv_freq]).reshape(1, LANES)
    return pl.pallas_call(
        _rope_table_kernel,
        grid=(t // tm,),
        in_specs=[pl.BlockSpec((tm, 1), lambda i: (i, 0)),
                  pl.BlockSpec((1, LANES), lambda i: (0, 0))],
        out_specs=[pl.BlockSpec((tm, LANES), lambda i: (i, 0)),
                   pl.BlockSpec((tm, LANES), lambda i: (i, 0))],
        out_shape=[jax.ShapeDtypeStruct((t, LANES), F32), jax.ShapeDtypeStruct((t, LANES), F32)],
        compiler_params=_params(1),
    )(positions.reshape(t, 1), freq)


def _diff_attn_kernel(qi_ref, ki_ref, q_ref, k_ref, v_ref, lam_ref, subln_ref, o_ref, m_ref, l_ref,
                      acc_ref, *, lam_init):
    p = pl.program_id(1)
    qi = qi_ref[p]
    ki = ki_ref[p]
    dh = DIFF_DH
    nt = (((1,), (1,)), ((), ()))

    @pl.when(ki == 0)
    def _():
        m_ref[...] = jnp.full(m_ref.shape, -jnp.inf, F32)
        l_ref[...] = jnp.zeros_like(l_ref)
        acc_ref[...] = jnp.zeros_like(acc_ref)

    def step(masked):
        v = v_ref[...]
        for mm in range(2):
            s = lax.dot_general(q_ref[:, mm * dh:(mm + 1) * dh], k_ref[:, mm * dh:(mm + 1) * dh], nt,
                                preferred_element_type=F32)
            if masked:
                row = lax.broadcasted_iota(jnp.int32, s.shape, 0)
                col = lax.broadcasted_iota(jnp.int32, s.shape, 1)
                s = jnp.where(row >= col, s, -jnp.inf)
            m_prev = m_ref[mm]
            m_new = jnp.maximum(m_prev, jnp.max(s, axis=1, keepdims=True))
            alpha = jnp.exp(m_prev - m_new)
            pr = jnp.exp(s - m_new)
            l_ref[mm] = alpha * l_ref[mm] + jnp.sum(pr, axis=1, keepdims=True)
            acc_ref[mm] = alpha * acc_ref[mm] + jnp.dot(pr.astype(BF16), v, preferred_element_type=F32)
            m_ref[mm] = m_new

    @pl.when(ki < qi)
    def _():
        step(False)

    @pl.when(ki == qi)
    def _():
        step(True)
        lf = lam_ref[...]
        d01 = jnp.sum(lf[0:1] * lf[1:2], axis=1, keepdims=True)
        d23 = jnp.sum(lf[2:3] * lf[3:4], axis=1, keepdims=True)
        lam = jnp.exp(d01) - jnp.exp(d23) + lam_init
        o = acc_ref[0] / l_ref[0] - lam * (acc_ref[1] / l_ref[1])
        o = _rms(o, subln_ref[...]) * (1.0 - lam_init)
        o_ref[...] = o.astype(o_ref.dtype)


def _diff_attn(qkv, lam, subln, *, tq, lam_init):
    t = qkv.shape[0]
    nq = t // tq
    hw = 2 * DIFF_DH
    qi_tab = jnp.asarray([qi for qi in range(nq) for _ in range(qi + 1)], jnp.int32)
    ki_tab = jnp.asarray([ki for qi in range(nq) for ki in range(qi + 1)], jnp.int32)
    k_off = C_QK // hw
    grid_spec = pltpu.PrefetchScalarGridSpec(
        num_scalar_prefetch=2,
        grid=(DIFF_HEADS, qi_tab.shape[0]),
        in_specs=[pl.BlockSpec((tq, hw), lambda h, p, qt, kt: (qt[p], h)),
                  pl.BlockSpec((tq, hw), lambda h, p, qt, kt: (kt[p], k_off + h)),
                  pl.BlockSpec((tq, hw), lambda h, p, qt, kt: (kt[p], 2 * k_off + h)),
                  pl.BlockSpec((4, DIFF_DH), lambda h, p, qt, kt: (0, 0)),
                  pl.BlockSpec((1, hw), lambda h, p, qt, kt: (0, 0))],
        out_specs=pl.BlockSpec((tq, hw), lambda h, p, qt, kt: (qt[p], h)),
        scratch_shapes=[pltpu.VMEM((2, tq, 1), F32), pltpu.VMEM((2, tq, 1), F32),
                        pltpu.VMEM((2, tq, hw), F32)])
    return pl.pallas_call(
        functools.partial(_diff_attn_kernel, lam_init=lam_init),
        grid_spec=grid_spec,
        out_shape=jax.ShapeDtypeStruct((t, DIFF_HEADS * hw), BF16),
        compiler_params=_params(2),
    )(qi_tab, ki_tab, qkv, qkv, qkv, lam, subln.reshape(1, hw))


def _router_kernel(h_ref, g_ref, rw_ref, rb_ref, xp_ref, ids_ref, gates_ref, rank_ref, cnt_ref,
                   carry_ref, *, n_experts):
    tm = h_ref.shape[0]
    half = h_ref.shape[1] // 2

    @pl.when(pl.program_id(0) == 0)
    def _():
        carry_ref[...] = jnp.zeros_like(carry_ref)

    xn = _rms(h_ref[...], g_ref[...])
    bits = lax.bitcast_convert_type(xn.astype(BF16).astype(F32), jnp.uint32)
    xp_ref[...] = (bits[:, :half] >> 16) | bits[:, half:]

    logits = jnp.dot(xn, rw_ref[...], precision=HI, preferred_element_type=F32) + rb_ref[...]
    lane = lax.broadcasted_iota(jnp.int32, (tm, LANES), 1)
    lane_f = lane.astype(F32)
    cur = jnp.where(lane < n_experts, logits, -jnp.inf)
    ids, vals = [], []
    for _ in range(TOP_K):
        m = jnp.max(cur, axis=1, keepdims=True)
        idx = jnp.min(jnp.where(cur == m, lane_f, float(LANES)), axis=1, keepdims=True)
        ids.append(idx)
        vals.append(m)
        cur = jnp.where(lane_f == idx, -jnp.inf, cur)
    exps = [jnp.exp(v - vals[0]) for v in vals]
    denom = exps[0] + exps[1] + exps[2] + exps[3]
    hot = [(lane_f == idx) for idx in ids]
    multi = jnp.zeros((tm, LANES), F32)
    for hk in hot:
        multi = multi + hk.astype(F32)
    row = lax.broadcasted_iota(jnp.int32, (tm, tm), 0)
    col = lax.broadcasted_iota(jnp.int32, (tm, tm), 1)
    strict = (row > col).astype(BF16)
    before = jnp.dot(strict, multi.astype(BF16), preferred_element_type=F32) + carry_ref[...]
    ids_out = jnp.zeros((tm, LANES), F32)
    gates_out = jnp.zeros((tm, LANES), F32)
    rank_out = jnp.zeros((tm, LANES), F32)
    for kk in range(TOP_K):
        rank = jnp.sum(jnp.where(hot[kk], before, 0.0), axis=1, keepdims=True)
        ids_out = jnp.where(lane == kk, ids[kk], ids_out)
        gates_out = jnp.where(lane == kk, exps[kk] / denom, gates_out)
        rank_out = jnp.where(lane == kk, rank, rank_out)
    ids_ref[...] = ids_out.astype(jnp.int32)
    gates_ref[...] = gates_out
    rank_ref[...] = rank_out.astype(jnp.int32)
    carry_ref[...] = carry_ref[...] + jnp.sum(multi, axis=0, keepdims=True)
    cnt_ref[...] = carry_ref[...]


def _router(h, gain, router_w, router_b, *, tm):
    t, d = h.shape
    e = router_w.shape[1]
    rw = jnp.pad(router_w, ((0, 0), (0, LANES - e)))
    rb = jnp.pad(router_b, (0, LANES - e)).reshape(1, LANES)
    row_spec = pl.BlockSpec((tm, LANES), lambda i: (i, 0))
    return pl.pallas_call(
        functools.partial(_router_kernel, n_experts=e),
        grid=(t // tm,),
        in_specs=[pl.BlockSpec((tm, d), lambda i: (i, 0)),
                  pl.BlockSpec((1, d), lambda i: (0, 0)),
                  pl.BlockSpec((d, LANES), lambda i: (0, 0)),
                  pl.BlockSpec((1, LANES), lambda i: (0, 0))],
        out_specs=[pl.BlockSpec((tm, d // 2), lambda i: (i, 0)), row_spec, row_spec, row_spec,
                   pl.BlockSpec((1, LANES), lambda i: (0, 0))],
        out_shape=[jax.ShapeDtypeStruct((t, d // 2), jnp.uint32),
                   jax.ShapeDtypeStruct((t, LANES), jnp.int32),
                   jax.ShapeDtypeStruct((t, LANES), F32),
                   jax.ShapeDtypeStruct((t, LANES), jnp.int32),
                   jax.ShapeDtypeStruct((1, LANES), F32)],
        scratch_shapes=[pltpu.VMEM((1, LANES), F32)],
        compiler_params=_params(1),
    )(h, gain.reshape(1, d), rw, rb)


def _row_copy(src_ref, dst_ref, sem, src_row, dst_row):
    return pltpu.make_async_copy(src_ref.at[pl.ds(src_row, 1)], dst_ref.at[pl.ds(dst_row, 1)], sem)


def _gather_rows_kernel(idx_ref, src_ref, dst_ref, sem, *, rows):
    base = pl.program_id(0) * rows

    def start(j, carry):
        _row_copy(src_ref, dst_ref, sem, idx_ref[0, 0, j], base + j).start()
        return carry

    def wait(j, carry):
        _row_copy(src_ref, dst_ref, sem, 0, base + j).wait()
        return carry

    lax.fori_loop(0, rows, start, 0)
    lax.fori_loop(0, rows, wait, 0)


def _gather_rows(src, idx, *, rows):
    n = idx.shape[0]
    return pl.pallas_call(
        functools.partial(_gather_rows_kernel, rows=rows),
        grid=(n // rows,),
        in_specs=[pl.BlockSpec((1, 1, rows), lambda i: (i, 0, 0), memory_space=pltpu.SMEM),
                  pl.BlockSpec(memory_space=pl.ANY)],
        out_specs=pl.BlockSpec(memory_space=pl.ANY),
        out_shape=jax.ShapeDtypeStruct((n, src.shape[1]), src.dtype),
        scratch_shapes=[pltpu.SemaphoreType.DMA(())],
        compiler_params=_params(1),
    )(idx.reshape(n // rows, 1, rows), src)


def _moe_up_kernel(xt_ref, e_ref, c_ref, ot_ref, oc_ref, first_ref, valid_ref, x_ref, wg_ref, wl_ref,
                   bg_ref, bl_ref, o_ref, wg_s, wl_s):
    i = pl.program_id(0)
    half = x_ref.shape[1]

    @pl.when(first_ref[i] == 1)
    def _():
        wg_s[...] = wg_ref[...].astype(BF16)
        wl_s[...] = wl_ref[...].astype(BF16)

    @pl.when(valid_ref[i] == 0)
    def _():
        o_ref[...] = jnp.zeros_like(o_ref)

    @pl.when(valid_ref[i] == 1)
    def _():
        words = x_ref[...]
        lo = lax.bitcast_convert_type(words << 16, F32).astype(BF16)
        hi = lax.bitcast_convert_type(words & jnp.uint32(0xFFFF0000), F32).astype(BF16)

        def proj(w_s, b_ref):
            return (jnp.dot(lo, w_s[:half, :], preferred_element_type=F32)
                    + jnp.dot(hi, w_s[half:, :], preferred_element_type=F32) + b_ref[...])

        x_glu = jnp.minimum(proj(wg_s, bg_ref), SWIGLU_LIMIT)
        x_lin = jnp.clip(proj(wl_s, bl_ref), -SWIGLU_LIMIT, SWIGLU_LIMIT)
        act = x_glu * jax.nn.sigmoid(SWIGLU_ALPHA * x_glu) * (x_lin + 1.0)
        o_ref[...] = act.astype(o_ref.dtype)


def _item_map(fn):
    return lambda i, xt, ex, ch, ot, oc, fi, va: fn(i, xt, ex, ch, ot, oc)


def _moe_up(x_sorted, w1, b1, tables, *, tm, fc):
    p_rows, half = x_sorted.shape
    e, d, ff2 = w1.shape
    ff = ff2 // 2
    ncb = ff // fc
    n_items = tables[0].shape[0]
    grid_spec = pltpu.PrefetchScalarGridSpec(
        num_scalar_prefetch=len(tables),
        grid=(n_items,),
        in_specs=[pl.BlockSpec((tm, half), _item_map(lambda i, xt, ex, ch, ot, oc: (xt[i], 0))),
                  pl.BlockSpec((None, d, fc), _item_map(lambda i, xt, ex, ch, ot, oc: (ex[i], 0, ch[i]))),
                  pl.BlockSpec((None, d, fc),
                               _item_map(lambda i, xt, ex, ch, ot, oc: (ex[i], 0, ncb + ch[i]))),
                  pl.BlockSpec((None, 1, fc), _item_map(lambda i, xt, ex, ch, ot, oc: (ex[i], 0, ch[i]))),
                  pl.BlockSpec((None, 1, fc),
                               _item_map(lambda i, xt, ex, ch, ot, oc: (ex[i], 0, ncb + ch[i])))],
        out_specs=pl.BlockSpec((tm, fc), _item_map(lambda i, xt, ex, ch, ot, oc: (ot[i], oc[i]))),
        scratch_shapes=[pltpu.VMEM((d, fc), BF16), pltpu.VMEM((d, fc), BF16)])
    b1r = b1.reshape(e, 1, ff2)
    return pl.pallas_call(
        _moe_up_kernel,
        grid_spec=grid_spec,
        out_shape=jax.ShapeDtypeStruct((p_rows, ff), BF16),
        compiler_params=_params(1),
    )(*tables, x_sorted, w1, w1, b1r, b1r)


def _moe_down_kernel(xt_ref, e_ref, c_ref, ot_ref, oc_ref, first_ref, valid_ref, a_ref, w_ref, b_ref,
                     o_ref, w_s):
    i = pl.program_id(0)

    @pl.when(first_ref[i] == 1)
    def _():
        w_s[...] = w_ref[...].astype(BF16)

    @pl.when(valid_ref[i] == 0)
    def _():
        o_ref[...] = jnp.zeros_like(o_ref)

    @pl.when(valid_ref[i] == 1)
    def _():
        o_ref[...] = jnp.dot(a_ref[...], w_s[...], preferred_element_type=F32) + b_ref[...]


def _moe_down(act, w2, b2, tables, *, tm, nc):
    p_rows, ff = act.shape
    e, _, d = w2.shape
    n_items = tables[0].shape[0]
    grid_spec = pltpu.PrefetchScalarGridSpec(
        num_scalar_prefetch=len(tables),
        grid=(n_items,),
        in_specs=[pl.BlockSpec((tm, ff), _item_map(lambda i, xt, ex, ch, ot, oc: (xt[i], 0))),
                  pl.BlockSpec((None, ff, nc), _item_map(lambda i, xt, ex, ch, ot, oc: (ex[i], 0, ch[i]))),
                  pl.BlockSpec((None, 1, nc), _item_map(lambda i, xt, ex, ch, ot, oc: (ex[i], 0, ch[i])))],
        out_specs=pl.BlockSpec((tm, nc), _item_map(lambda i, xt, ex, ch, ot, oc: (ot[i], oc[i]))),
        scratch_shapes=[pltpu.VMEM((ff, nc), BF16)])
    return pl.pallas_call(
        _moe_down_kernel,
        grid_spec=grid_spec,
        out_shape=jax.ShapeDtypeStruct((p_rows, d), F32),
        compiler_params=_params(1),
    )(*tables, act, w2, b2.reshape(e, 1, d))


def _item_tables(tiles_per_e, tile_start_e, n_tiles_max, n_chunks):
    n_items = n_tiles_max * n_chunks
    items_per_e = tiles_per_e * n_chunks
    item_end = jnp.cumsum(items_per_e)
    item_start = item_end - items_per_e
    n_valid = item_end[-1]
    n_used_tiles = n_valid // n_chunks
    i = jnp.arange(n_items, dtype=jnp.int32)
    valid = i < n_valid
    ic = jnp.minimum(i, n_valid - 1)
    ex = jnp.minimum(jnp.searchsorted(item_end, ic, side="right"), tiles_per_e.shape[0] - 1)
    ex = ex.astype(jnp.int32)
    local = ic - item_start[ex]
    per = jnp.maximum(tiles_per_e[ex], 1)
    ch = local // per
    r_local = local % per
    tile = tile_start_e[ex] + r_local
    spare = i - n_valid
    out_tile = jnp.where(valid, tile, n_used_tiles + spare // n_chunks)
    out_chunk = jnp.where(valid, ch, spare % n_chunks)
    first = valid & (r_local == 0)
    return tuple(a.astype(jnp.int32) for a in (tile, ex, ch, out_tile, out_chunk, first, valid))


def _post_kernel(pos_ref, h_ref, gates_ref, y_ref, p_ref, pn_ref, pg_ref, pp_ref, fn_ref, o_ref, ybuf,
                 sem, *, final):
    tm = h_ref.shape[0]

    def copy(j, k, row):
        return pltpu.make_async_copy(y_ref.at[pl.ds(row, 1)], ybuf.at[k, pl.ds(j, 1)], sem)

    def start(j, carry):
        for k in range(TOP_K):
            copy(j, k, pos_ref[0, 0, j * TOP_K + k]).start()
        return carry

    def wait(j, carry):
        for k in range(TOP_K):
            copy(j, k, 0).wait()
        return carry

    lax.fori_loop(0, tm, start, 0)
    lax.fori_loop(0, tm, wait, 0)

    gates = gates_ref[...]
    h2 = h_ref[...]
    for k in range(TOP_K):
        h2 = h2 + gates[:, k:k + 1] * ybuf[k]
    hn = _rms(h2, pn_ref[...]).astype(BF16)
    gate = jax.nn.sigmoid(jnp.dot(hn, pg_ref[...], preferred_element_type=F32))
    pe = jnp.dot(p_ref[...].astype(BF16), pp_ref[...], preferred_element_type=F32)
    h3 = h2 + pe * gate
    if final:
        h3 = _rms(h3, fn_ref[...])
    o_ref[...] = h3


def _post(h, pos, gates, y, p, ple_norm, ple_gate, ple_proj, final_norm, *, tm, final):
    t, d = h.shape
    pd = p.shape[1]
    const = lambda i: (0, 0)
    return pl.pallas_call(
        functools.partial(_post_kernel, final=final),
        grid=(t // tm,),
        in_specs=[pl.BlockSpec((1, 1, tm * TOP_K), lambda i: (i, 0, 0), memory_space=pltpu.SMEM),
                  pl.BlockSpec((tm, d), lambda i: (i, 0)),
                  pl.BlockSpec((tm, LANES), lambda i: (i, 0)),
                  pl.BlockSpec(memory_space=pl.ANY),
                  pl.BlockSpec((tm, pd), lambda i: (i, 0)),
                  pl.BlockSpec((1, d), const),
                  pl.BlockSpec((d, d), const),
                  pl.BlockSpec((pd, d), const),
                  pl.BlockSpec((1, d), const)],
        out_specs=pl.BlockSpec((tm, d), lambda i: (i, 0)),
        out_shape=jax.ShapeDtypeStruct((t, d), F32),
        scratch_shapes=[pltpu.VMEM((TOP_K, tm, d), F32), pltpu.SemaphoreType.DMA(())],
        compiler_params=_params(1),
    )(pos.reshape(t // tm, 1, tm * TOP_K), h, gates, y, p, ple_norm.reshape(1, d), ple_gate, ple_proj,
      final_norm.reshape(1, d))


def _moe_ple(h, p, ffn_norm, router_w, router_b, w1, b1, w2, b2, ple_norm, ple_gate, ple_proj,
             final_norm, *, final, tiles):
    t, d = h.shape
    e = router_w.shape[1]
    tm_r, tm_e, fc, nc, tm_p, g_rows = tiles
    xp, ids, gates, rank, cnt = _router(h, ffn_norm, router_w, router_b, tm=tm_r)

    counts = cnt[0, :e].astype(jnp.int32)
    tiles_per_e = (counts + tm_e - 1) // tm_e
    padded = tiles_per_e * tm_e
    ends = jnp.cumsum(padded)
    starts = ends - padded
    ids4 = ids[:, :TOP_K]
    pos = starts[ids4] + rank[:, :TOP_K]
    n_tiles_max = (t * TOP_K) // tm_e + e
    p_rows = n_tiles_max * tm_e
    tok = jnp.broadcast_to(jnp.arange(t, dtype=jnp.int32)[:, None], (t, TOP_K))
    sorted_tok = jnp.zeros((p_rows,), jnp.int32).at[pos.reshape(-1)].set(tok.reshape(-1))
    tile_start_e = starts // tm_e

    x_sorted = _gather_rows(xp, sorted_tok, rows=g_rows)
    ff = w1.shape[2] // 2
    act = _moe_up(x_sorted, w1, b1, _item_tables(tiles_per_e, tile_start_e, n_tiles_max, ff // fc),
                  tm=tm_e, fc=fc)
    y = _moe_down(act, w2, b2, _item_tables(tiles_per_e, tile_start_e, n_tiles_max, d // nc),
                  tm=tm_e, nc=nc)
    return _post(h, pos, gates, y, p, ple_norm, ple_gate.astype(BF16), ple_proj.astype(BF16),
                 final_norm, tm=tm_p, final=final)


def _pick(t, pref):
    return min(pref, t)


def kernel(x, p, positions, mix_norm, ffn_norm, ev_w_in, ev_conv_w, ev_a_log, ev_dt_bias, ev_out_norm, ev_sgu_ln_g, ev_sgu_ln_b, ev_sgu_w, ev_sgu_b, ev_w_out, od_w_in, od_lambda, od_subln, od_w_out, router_w, router_b, exp_w1, exp_b1, exp_w2, exp_b2, ple_proj, ple_norm, ple_gate, final_norm):
    bsz, s, d = x.shape
    depth = mix_norm.shape[0]
    t = bsz * s
    assert bsz == 1, "sequence mixers here assume one sequence"
    h = x.reshape(t, d)
    pos1 = positions.reshape(t)

    tm = _pick(t, 512)
    tiles = (_pick(t, 512), 256, 512, 1024, _pick(t, 256), _pick(t, 512))
    rope = None

    for i in range(depth):
        j = i // 2
        if i % 2 == 0:
            w_in = ev_w_in[j]
            ab_cols = 2 * GDN_HEADS
            g_off = A_QKV + A_V + ab_cols
            w_cat = jnp.concatenate(
                [w_in[:, :A_QKV + A_V], w_in[:, g_off:],
                 jnp.pad(w_in[:, A_QKV + A_V:g_off], ((0, 0), (0, LANES - ab_cols)))], axis=1)
            proj = _norm_matmul(h, mix_norm[i], w_cat.astype(BF16), tm=tm, tn=LANES * 7, out_dtype=F32)
            gu_block = (A_QKV + A_V) // LANES
            gv_block = gu_block + B_W // LANES
            ab_block = gv_block + B_W // LANES
            out_a = _gdn(proj, ev_conv_w[j], ev_a_log[j], ev_dt_bias[j], ev_out_norm[j],
                         tb=_pick(t, 512), ab_block=ab_block)
            out_b = _sgu(proj, ev_sgu_ln_g[j], ev_sgu_ln_b[j], ev_sgu_w[j], ev_sgu_b[j],
                         tb=_pick(t, 512), gu_block=gu_block, gv_block=gv_block)
            w_out = ev_w_out[j].astype(BF16)
            h = _matmul_residual([out_a, out_b], [w_out[:A_V], w_out[A_V:]], h, tm=tm, tn=1024)
        else:
            if rope is None:
                rope = _rope_tables(pos1, tm=tm)
            lam_init = 0.8 - 0.6 * math.exp(-0.3 * i)
            qkv = _norm_matmul(h, mix_norm[i], od_w_in[j].astype(BF16), tm=tm, tn=512, out_dtype=BF16,
                               rope=(rope[0], rope[1], C_QK, 2 * C_QK, DIFF_DH ** -0.5))
            att = _diff_attn(qkv, od_lambda[j], od_subln[j], tq=_pick(t, 512), lam_init=lam_init)
            h = _matmul_residual([att], [od_w_out[j].astype(BF16)], h, tm=tm, tn=1024)
        h = _moe_ple(h, p[i].reshape(t, -1), ffn_norm[i], router_w[i], router_b[i], exp_w1[i], exp_b1[i],
                     exp_w2[i], exp_b2[i], ple_norm[i], ple_gate[i], ple_proj[i], final_norm,
                     final=(i == depth - 1), tiles=tiles)
    return h.reshape(bsz, s, d)
```

```python
import functools
import math

import jax
import jax.numpy as jnp
from jax import lax
from jax.experimental import pallas as pl
from jax.experimental.pallas import tpu as pltpu

F32 = jnp.float32
BF16 = jnp.bfloat16
HI = lax.Precision.HIGHEST

NORM_EPS = 1e-6
L2_EPS = 1e-6
LN_EPS = 1e-5
LANES = 128
SUBLANES = 8
VMEM_LIMIT = 56 * 1024 * 1024

GDN_HEADS = 8
GDN_DK = 128
GDN_DV = 128
GDN_CONV = 4
GDN_CHUNK = 128
SGU_GROUPS = 8
SGU_DIM = 128
SGU_CHUNK = 128
DIFF_HEADS = 8
DIFF_DH = 128
ROPE_THETA = 10000.0
TOP_K = 4
SWIGLU_ALPHA = 1.702
SWIGLU_LIMIT = 7.0

A_QK = GDN_HEADS * GDN_DK
A_V = GDN_HEADS * GDN_DV
A_QKV = 2 * A_QK + A_V
B_W = SGU_GROUPS * SGU_DIM
C_QK = DIFF_HEADS * 2 * DIFF_DH


def _params(n_axes):
    return pltpu.CompilerParams(dimension_semantics=("arbitrary",) * n_axes,
                                vmem_limit_bytes=VMEM_LIMIT)


def _rms(x, g):
    return x * lax.rsqrt(jnp.mean(x * x, axis=-1, keepdims=True) + NORM_EPS) * g


def _lane_tile(x, reps):
    return jnp.concatenate([x] * reps, axis=1)


def _split_bf16(x):
    hi = x.astype(BF16)
    return hi, (x - hi.astype(F32)).astype(BF16)


def _dot3(a, b):
    ah, al = _split_bf16(a)
    bh, bl = _split_bf16(b)
    return (jnp.dot(ah, bh, preferred_element_type=F32)
            + (jnp.dot(ah, bl, preferred_element_type=F32) + jnp.dot(al, bh, preferred_element_type=F32)))


def _dot_exact_lhs(a_bf, b):
    b0 = b.astype(BF16)
    r = b - b0.astype(F32)
    b1 = r.astype(BF16)
    b2 = (r - b1.astype(F32)).astype(BF16)
    return (jnp.dot(a_bf, b0, preferred_element_type=F32)
            + (jnp.dot(a_bf, b1, preferred_element_type=F32) + jnp.dot(a_bf, b2, preferred_element_type=F32)))


def _gelu(x):
    return 0.5 * x * (1.0 + lax.erf(x * (2.0 ** -0.5)))


def _softplus(x):
    return jnp.maximum(x, 0.0) + jnp.log1p(jnp.exp(-jnp.abs(x)))


def _norm_matmul_kernel(x_ref, g_ref, w_ref, o_ref, xn_ref):
    @pl.when(pl.program_id(1) == 0)
    def _():
        xn_ref[...] = _rms(x_ref[...], g_ref[...]).astype(BF16)

    o_ref[...] = jnp.dot(xn_ref[...], w_ref[...], preferred_element_type=F32).astype(o_ref.dtype)


def _norm_matmul_rope_kernel(x_ref, g_ref, w_ref, cos_ref, sin_ref, o_ref, xn_ref, *, n_q_blocks,
                             n_rope_blocks, q_scale):
    j = pl.program_id(1)

    @pl.when(j == 0)
    def _():
        xn_ref[...] = _rms(x_ref[...], g_ref[...]).astype(BF16)

    y = jnp.dot(xn_ref[...], w_ref[...], preferred_element_type=F32)

    @pl.when(j < n_rope_blocks)
    def _():
        cos = cos_ref[...]
        sin = sin_ref[...]
        scale = jnp.where(j < n_q_blocks, q_scale, 1.0).astype(F32)
        for c in range(y.shape[1] // LANES):
            yc = y[:, c * LANES:(c + 1) * LANES]
            rot = pltpu.roll(yc, LANES // 2, axis=1)
            o_ref[:, c * LANES:(c + 1) * LANES] = ((yc * cos + rot * sin) * scale).astype(o_ref.dtype)

    @pl.when(j >= n_rope_blocks)
    def _():
        o_ref[...] = y.astype(o_ref.dtype)


def _norm_matmul(x, gain, w, *, tm, tn, out_dtype, rope=None):
    t, d = x.shape
    n = w.shape[1]
    grid = (t // tm, n // tn)
    in_specs = [pl.BlockSpec((tm, d), lambda i, j: (i, 0)),
                pl.BlockSpec((1, d), lambda i, j: (0, 0)),
                pl.BlockSpec((d, tn), lambda i, j: (0, j))]
    args = [x, gain.reshape(1, d), w]
    if rope is None:
        body = _norm_matmul_kernel
    else:
        cos, sin, n_q_cols, n_rope_cols, q_scale = rope
        body = functools.partial(_norm_matmul_rope_kernel, n_q_blocks=n_q_cols // tn,
                                 n_rope_blocks=n_rope_cols // tn, q_scale=q_scale)
        in_specs += [pl.BlockSpec((tm, LANES), lambda i, j: (i, 0)),
                     pl.BlockSpec((tm, LANES), lambda i, j: (i, 0))]
        args += [cos, sin]
    return pl.pallas_call(
        body,
        grid=grid,
        in_specs=in_specs,
        out_specs=pl.BlockSpec((tm, tn), lambda i, j: (i, j)),
        out_shape=jax.ShapeDtypeStruct((t, n), out_dtype),
        scratch_shapes=[pltpu.VMEM((tm, d), BF16)],
        compiler_params=_params(2),
    )(*args)


def _matmul_residual_kernel(*refs, n_acts):
    a_refs = refs[:n_acts]
    w_refs = refs[n_acts:2 * n_acts]
    res_ref = refs[2 * n_acts]
    o_ref = refs[2 * n_acts + 1]
    acc = res_ref[...]
    for a_ref, w_ref in zip(a_refs, w_refs):
        acc = acc + jnp.dot(a_ref[...], w_ref[...], preferred_element_type=F32)
    o_ref[...] = acc


def _matmul_residual(acts, ws, res, *, tm, tn):
    t, n = res.shape
    n_acts = len(acts)
    in_specs = [pl.BlockSpec((tm, a.shape[1]), lambda i, j: (i, 0)) for a in acts]
    in_specs += [pl.BlockSpec((w.shape[0], tn), lambda i, j: (0, j)) for w in ws]
    in_specs += [pl.BlockSpec((tm, tn), lambda i, j: (i, j))]
    return pl.pallas_call(
        functools.partial(_matmul_residual_kernel, n_acts=n_acts),
        grid=(t // tm, n // tn),
        in_specs=in_specs,
        out_specs=pl.BlockSpec((tm, tn), lambda i, j: (i, j)),
        out_shape=jax.ShapeDtypeStruct((t, n), F32),
        compiler_params=_params(2),
    )(*acts, *ws, res)


def _gdn_kernel(q_ref, k_ref, v_ref, z_ref, ab_ref, cwq_ref, cwk_ref, cwv_ref, alog_ref, dtb_ref,
                onorm_ref, o_ref, buf_ref, s_ref, *, tb):
    c_len = GDN_CHUNK
    h = pl.program_id(0)
    halo = SUBLANES

    @pl.when(pl.program_id(1) == 0)
    def _():
        buf_ref[:, 0:halo, :] = jnp.zeros((3, halo, LANES), F32)
        s_ref[...] = jnp.zeros_like(s_ref)

    def conv_silu(i, x_ref, cw_ref):
        x = x_ref[...]
        buf_ref[i, halo:halo + tb, :] = x
        cw = cw_ref[...]
        acc = x * cw[GDN_CONV - 1:GDN_CONV, :]
        for s in range(1, GDN_CONV):
            acc = acc + buf_ref[i, halo - s:halo - s + tb, :] * cw[GDN_CONV - 1 - s:GDN_CONV - s, :]
        buf_ref[i, 0:halo, :] = x[tb - halo:tb, :]
        return acc * jax.nn.sigmoid(acc)

    q = conv_silu(0, q_ref, cwq_ref)
    k = conv_silu(1, k_ref, cwk_ref)
    v = conv_silu(2, v_ref, cwv_ref)
    q = q * lax.rsqrt(jnp.sum(q * q, axis=-1, keepdims=True) + L2_EPS) * (GDN_DK ** -0.5)
    k = k * lax.rsqrt(jnp.sum(k * k, axis=-1, keepdims=True) + L2_EPS)

    ab = ab_ref[...]
    lane = lax.broadcasted_iota(jnp.int32, (1, LANES), 1)
    g_all = -jnp.exp(alog_ref[...]) * _softplus(ab + dtb_ref[...])
    g_col = jnp.sum(jnp.where(lane == h, g_all, 0.0), axis=1, keepdims=True)
    beta = jnp.sum(jnp.where(lane == GDN_HEADS + h, jax.nn.sigmoid(ab), 0.0), axis=1, keepdims=True)

    row = lax.broadcasted_iota(jnp.int32, (c_len, c_len), 0)
    col = lax.broadcasted_iota(jnp.int32, (c_len, c_len), 1)
    tri_incl_bf = (row >= col).astype(BF16)
    eye = (row == col).astype(F32)
    onorm = onorm_ref[...]
    nt = (((1,), (1,)), ((), ()))

    for c in range(tb // c_len):
        sl = slice(c * c_len, (c + 1) * c_len)
        qc, kc, vc, bc = q[sl], k[sl], v[sl], beta[sl]
        gb = _dot_exact_lhs(tri_incl_bf, jnp.broadcast_to(g_col[sl], (c_len, LANES)))
        dif = gb - gb.T
        gam = jnp.where(row >= col, jnp.exp(jnp.minimum(dif, 0.0)), 0.0)
        kb = kc * bc
        sc = lax.dot_general(jnp.concatenate([kb, qc], axis=0).astype(BF16), kc.astype(BF16), nt,
                             preferred_element_type=F32)
        low = jnp.where(row > col, sc[:c_len] * gam, 0.0)
        aqk = sc[c_len:] * gam
        inv = eye - low
        pw = low
        for _ in range(int(math.log2(c_len)) - 1):
            pw = _dot3(pw, pw)
            inv = inv + _dot3(inv, pw)
        eg = jnp.exp(gb)
        sol = _dot3(inv, jnp.concatenate([vc * bc, kb * eg], axis=1))
        u, w = sol[:, :GDN_DV], sol[:, GDN_DV:]
        g_last = gb[c_len - 1:c_len, :]
        qd = qc * eg
        kd = kc * jnp.exp(g_last - gb)
        state = s_ref[...]
        ws = jnp.dot(jnp.concatenate([w, qd], axis=0).astype(BF16), state.astype(BF16),
                     preferred_element_type=F32)
        v_new = u - ws[:c_len]
        v_new_bf = v_new.astype(BF16)
        o = ws[c_len:] + jnp.dot(aqk.astype(BF16), v_new_bf, preferred_element_type=F32)
        s_ref[...] = state * jnp.exp(g_last) + jnp.dot(kd.T.astype(BF16), v_new_bf,
                                                        preferred_element_type=F32)
        zc = z_ref[sl, :]
        o = _rms(o, onorm) * (zc * jax.nn.sigmoid(zc))
        o_ref[sl, :] = o.astype(o_ref.dtype)


def _gdn(proj, conv_w, a_log, dt_bias, out_norm, *, tb, ab_block):
    t = proj.shape[0]
    hds = GDN_HEADS

    def col(off):
        return pl.BlockSpec((tb, LANES), lambda h, b, off=off: (b, off + h))

    def cw(off):
        return pl.BlockSpec((GDN_CONV, LANES), lambda h, b, off=off: (0, off + h))

    vec = pl.BlockSpec((1, LANES), lambda h, b: (0, 0))
    pad = LANES - hds
    return pl.pallas_call(
        functools.partial(_gdn_kernel, tb=tb),
        grid=(hds, t // tb),
        in_specs=[col(0), col(hds), col(2 * hds), col(3 * hds),
                  pl.BlockSpec((tb, LANES), lambda h, b: (b, ab_block)),
                  cw(0), cw(hds), cw(2 * hds), vec, vec, vec],
        out_specs=pl.BlockSpec((tb, LANES), lambda h, b: (b, h)),
        out_shape=jax.ShapeDtypeStruct((t, A_V), BF16),
        scratch_shapes=[pltpu.VMEM((3, SUBLANES + tb, LANES), F32),
                        pltpu.VMEM((GDN_DK, GDN_DV), F32)],
        compiler_params=_params(2),
    )(proj, proj, proj, proj, proj, conv_w, conv_w, conv_w,
      jnp.pad(a_log, (0, pad)).reshape(1, LANES), jnp.pad(dt_bias, (0, pad)).reshape(1, LANES),
      out_norm.reshape(1, LANES))


def _sgu_kernel(gu_ref, gv_ref, lng_ref, lnb_ref, w_ref, b_ref, o_ref, *, tb):
    c_len = SGU_CHUNK
    row = lax.broadcasted_iota(jnp.int32, (c_len, c_len), 0)
    col = lax.broadcasted_iota(jnp.int32, (c_len, c_len), 1)
    w = jnp.where(row >= col, w_ref[...], 0.0)
    bias = b_ref[...]
    gu = _gelu(gu_ref[...])
    gv = _gelu(gv_ref[...])
    mu = jnp.mean(gv, axis=-1, keepdims=True)
    xc = gv - mu
    var = jnp.mean(xc * xc, axis=-1, keepdims=True)
    gv = xc * lax.rsqrt(var + LN_EPS) * lng_ref[...] + lnb_ref[...]
    for c in range(tb // c_len):
        sl = slice(c * c_len, (c + 1) * c_len)
        mixed = jnp.dot(w, gv[sl], precision=HI, preferred_element_type=F32) + bias
        o_ref[sl, :] = (gu[sl] * mixed).astype(o_ref.dtype)


def _sgu(proj, ln_g, ln_b, sgu_w, sgu_b, *, tb, gu_block, gv_block):
    t = proj.shape[0]
    g = SGU_GROUPS
    return pl.pallas_call(
        functools.partial(_sgu_kernel, tb=tb),
        grid=(t // tb, g),
        in_specs=[pl.BlockSpec((tb, LANES), lambda b, gi: (b, gu_block + gi)),
                  pl.BlockSpec((tb, LANES), lambda b, gi: (b, gv_block + gi)),
                  pl.BlockSpec((None, 1, SGU_DIM), lambda b, gi: (gi, 0, 0)),
                  pl.BlockSpec((None, 1, SGU_DIM), lambda b, gi: (gi, 0, 0)),
                  pl.BlockSpec((None, SGU_CHUNK, SGU_CHUNK), lambda b, gi: (gi, 0, 0)),
                  pl.BlockSpec((None, SGU_CHUNK, 1), lambda b, gi: (gi, 0, 0))],
        out_specs=pl.BlockSpec((tb, LANES), lambda b, gi: (b, gi)),
        out_shape=jax.ShapeDtypeStruct((t, B_W), BF16),
        compiler_params=_params(2),
    )(proj, proj, ln_g.reshape(g, 1, SGU_DIM), ln_b.reshape(g, 1, SGU_DIM), sgu_w,
      sgu_b.reshape(g, SGU_CHUNK, 1))


def _rope_table_kernel(pos_ref, freq_ref, cos_ref, sin_ref):
    ang = pos_ref[...].astype(F32) * freq_ref[...]
    lane = lax.broadcasted_iota(jnp.int32, ang.shape, 1)
    cos_ref[...] = jnp.cos(ang)
    sin_ref[...] = jnp.where(lane < LANES // 2, -1.0, 1.0) * jnp.sin(ang)


def _rope_tables(positions, *, tm):
    t = positions.shape[0]
    half = DIFF_DH // 2
    inv_freq = ROPE_THETA ** (-jnp.arange(half, dtype=F32) / half)
    freq = jnp.concatenate([inv_freq, inv_freq]).reshape(1, LANES)
    return pl.pallas_call(
        _rope_table_kernel,
        grid=(t // tm,),
        in_specs=[pl.BlockSpec((tm, 1), lambda i: (i, 0)),
                  pl.BlockSpec((1, LANES), lambda i: (0, 0))],
        out_specs=[pl.BlockSpec((tm, LANES), lambda i: (i, 0)),
                   pl.BlockSpec((tm, LANES), lambda i: (i, 0))],
        out_shape=[jax.ShapeDtypeStruct((t, LANES), F32), jax.ShapeDtypeStruct((t, LANES), F32)],
        compiler_params=_params(1),
    )(positions.reshape(t, 1), freq)


def _diff_attn_kernel(qi_ref, ki_ref, q_ref, k_ref, v_ref, lam_ref, subln_ref, o_ref, m_ref, l_ref,
                      acc_ref, *, lam_init):
    p = pl.program_id(1)
    qi = qi_ref[p]
    ki = ki_ref[p]
    dh = DIFF_DH
    nt = (((1,), (1,)), ((), ()))

    @pl.when(ki == 0)
    def _():
        m_ref[...] = jnp.full(m_ref.shape, -jnp.inf, F32)
        l_ref[...] = jnp.zeros_like(l_ref)
        acc_ref[...] = jnp.zeros_like(acc_ref)

    def step(masked):
        v = v_ref[...]
        for mm in range(2):
            s = lax.dot_general(q_ref[:, mm * dh:(mm + 1) * dh], k_ref[:, mm * dh:(mm + 1) * dh], nt,
                                preferred_element_type=F32)
            if masked:
                row = lax.broadcasted_iota(jnp.int32, s.shape, 0)
                col = lax.broadcasted_iota(jnp.int32, s.shape, 1)
                s = jnp.where(row >= col, s, -jnp.inf)
            m_prev = m_ref[mm]
            m_new = jnp.maximum(m_prev, jnp.max(s, axis=1, keepdims=True))
            alpha = jnp.exp(m_prev - m_new)
            pr = jnp.exp(s - _lane_tile(m_new, s.shape[1] // LANES))
            l_ref[mm] = alpha * l_ref[mm] + jnp.sum(pr, axis=1, keepdims=True)
            acc_ref[mm] = (_lane_tile(alpha, acc_ref.shape[2] // LANES) * acc_ref[mm]
                           + jnp.dot(pr.astype(BF16), v, preferred_element_type=F32))
            m_ref[mm] = m_new

    @pl.when(ki < qi)
    def _():
        step(False)

    @pl.when(ki == qi)
    def _():
        step(True)
        lf = lam_ref[...]
        d01 = jnp.sum(lf[0:1] * lf[1:2], axis=1, keepdims=True)
        d23 = jnp.sum(lf[2:3] * lf[3:4], axis=1, keepdims=True)
        lam = jnp.exp(d01) - jnp.exp(d23) + lam_init
        reps = acc_ref.shape[2] // LANES
        o = (acc_ref[0] / _lane_tile(l_ref[0], reps)
             - lam * (acc_ref[1] / _lane_tile(l_ref[1], reps)))
        o = _rms(o, subln_ref[...]) * (1.0 - lam_init)
        o_ref[...] = o.astype(o_ref.dtype)


def _diff_attn(qkv, lam, subln, *, tq, lam_init):
    t = qkv.shape[0]
    nq = t // tq
    hw = 2 * DIFF_DH
    qi_tab = jnp.asarray([qi for qi in range(nq) for _ in range(qi + 1)], jnp.int32)
    ki_tab = jnp.asarray([ki for qi in range(nq) for ki in range(qi + 1)], jnp.int32)
    k_off = C_QK // hw
    grid_spec = pltpu.PrefetchScalarGridSpec(
        num_scalar_prefetch=2,
        grid=(DIFF_HEADS, qi_tab.shape[0]),
        in_specs=[pl.BlockSpec((tq, hw), lambda h, p, qt, kt: (qt[p], h)),
                  pl.BlockSpec((tq, hw), lambda h, p, qt, kt: (kt[p], k_off + h)),
                  pl.BlockSpec((tq, hw), lambda h, p, qt, kt: (kt[p], 2 * k_off + h)),
                  pl.BlockSpec((4, DIFF_DH), lambda h, p, qt, kt: (0, 0)),
                  pl.BlockSpec((1, hw), lambda h, p, qt, kt: (0, 0))],
        out_specs=pl.BlockSpec((tq, hw), lambda h, p, qt, kt: (qt[p], h)),
        scratch_shapes=[pltpu.VMEM((2, tq, LANES), F32), pltpu.VMEM((2, tq, LANES), F32),
                        pltpu.VMEM((2, tq, hw), F32)])
    return pl.pallas_call(
        functools.partial(_diff_attn_kernel, lam_init=lam_init),
        grid_spec=grid_spec,
        out_shape=jax.ShapeDtypeStruct((t, DIFF_HEADS * hw), BF16),
        compiler_params=_params(2),
    )(qi_tab, ki_tab, qkv, qkv, qkv, lam, subln.reshape(1, hw))


def _router_kernel(h_ref, g_ref, rw_ref, rb_ref, xp_ref, ids_ref, gates_ref, rank_ref, cnt_ref,
                   carry_ref, *, n_experts):
    tm = h_ref.shape[0]
    half = h_ref.shape[1] // 2

    @pl.when(pl.program_id(0) == 0)
    def _():
        carry_ref[...] = jnp.zeros_like(carry_ref)

    xn = _rms(h_ref[...], g_ref[...])
    bits = lax.bitcast_convert_type(xn.astype(BF16).astype(F32), jnp.uint32)
    xp_ref[...] = (bits[:, :half] >> 16) | bits[:, half:]

    logits = jnp.dot(xn, rw_ref[...], precision=HI, preferred_element_type=F32) + rb_ref[...]
    lane = lax.broadcasted_iota(jnp.int32, (tm, LANES), 1)
    lane_f = lane.astype(F32)
    cur = jnp.where(lane < n_experts, logits, -jnp.inf)
    ids, vals = [], []
    for _ in range(TOP_K):
        m = jnp.max(cur, axis=1, keepdims=True)
        idx = jnp.min(jnp.where(cur == m, lane_f, float(LANES)), axis=1, keepdims=True)
        ids.append(idx)
        vals.append(m)
        cur = jnp.where(lane_f == idx, -jnp.inf, cur)
    exps = [jnp.exp(v - vals[0]) for v in vals]
    denom = exps[0] + exps[1] + exps[2] + exps[3]
    hot = [(lane_f == idx) for idx in ids]
    multi = jnp.zeros((tm, LANES), F32)
    for hk in hot:
        multi = multi + hk.astype(F32)
    row = lax.broadcasted_iota(jnp.int32, (tm, tm), 0)
    col = lax.broadcasted_iota(jnp.int32, (tm, tm), 1)
    strict = (row > col).astype(BF16)
    before = jnp.dot(strict, multi.astype(BF16), preferred_element_type=F32) + carry_ref[...]
    ids_out = jnp.zeros((tm, LANES), F32)
    gates_out = jnp.zeros((tm, LANES), F32)
    rank_out = jnp.zeros((tm, LANES), F32)
    for kk in range(TOP_K):
        rank = jnp.sum(jnp.where(hot[kk], before, 0.0), axis=1, keepdims=True)
        ids_out = jnp.where(lane == kk, ids[kk], ids_out)
        gates_out = jnp.where(lane == kk, exps[kk] / denom, gates_out)
        rank_out = jnp.where(lane == kk, rank, rank_out)
    ids_ref[...] = ids_out.astype(jnp.int32)
    gates_ref[...] = gates_out
    rank_ref[...] = rank_out.astype(jnp.int32)
    carry_ref[...] = carry_ref[...] + jnp.sum(multi, axis=0, keepdims=True)
    cnt_ref[...] = carry_ref[...]


def _router(h, gain, router_w, router_b, *, tm):
    t, d = h.shape
    e = router_w.shape[1]
    rw = jnp.pad(router_w, ((0, 0), (0, LANES - e)))
    rb = jnp.pad(router_b, (0, LANES - e)).reshape(1, LANES)
    row_spec = pl.BlockSpec((tm, LANES), lambda i: (i, 0))
    return pl.pallas_call(
        functools.partial(_router_kernel, n_experts=e),
        grid=(t // tm,),
        in_specs=[pl.BlockSpec((tm, d), lambda i: (i, 0)),
                  pl.BlockSpec((1, d), lambda i: (0, 0)),
                  pl.BlockSpec((d, LANES), lambda i: (0, 0)),
                  pl.BlockSpec((1, LANES), lambda i: (0, 0))],
        out_specs=[pl.BlockSpec((tm, d // 2), lambda i: (i, 0)), row_spec, row_spec, row_spec,
                   pl.BlockSpec((1, LANES), lambda i: (0, 0))],
        out_shape=[jax.ShapeDtypeStruct((t, d // 2), jnp.uint32),
                   jax.ShapeDtypeStruct((t, LANES), jnp.int32),
                   jax.ShapeDtypeStruct((t, LANES), F32),
                   jax.ShapeDtypeStruct((t, LANES), jnp.int32),
                   jax.ShapeDtypeStruct((1, LANES), F32)],
        scratch_shapes=[pltpu.VMEM((1, LANES), F32)],
        compiler_params=_params(1),
    )(h, gain.reshape(1, d), rw, rb)


def _row_copy(src_ref, dst_ref, sem, src_row, dst_row):
    return pltpu.make_async_copy(src_ref.at[pl.ds(src_row, 1)], dst_ref.at[pl.ds(dst_row, 1)], sem)


def _gather_rows_kernel(idx_ref, src_ref, o_ref, buf_ref, sem, *, rows):
    def start(j, carry):
        _row_copy(src_ref, buf_ref, sem, idx_ref[0, 0, j], j).start()
        return carry

    def wait(j, carry):
        _row_copy(src_ref, buf_ref, sem, 0, j).wait()
        return carry

    lax.fori_loop(0, rows, start, 0, unroll=8)
    lax.fori_loop(0, rows, wait, 0, unroll=8)
    o_ref[...] = buf_ref[...]


def _gather_rows(src, idx, *, rows):
    n = idx.shape[0]
    width = src.shape[1]
    return pl.pallas_call(
        functools.partial(_gather_rows_kernel, rows=rows),
        grid=(n // rows,),
        in_specs=[pl.BlockSpec((1, 1, rows), lambda i: (i, 0, 0), memory_space=pltpu.SMEM),
                  pl.BlockSpec(memory_space=pl.ANY)],
        out_specs=pl.BlockSpec((rows, width), lambda i: (i, 0)),
        out_shape=jax.ShapeDtypeStruct((n, width), src.dtype),
        scratch_shapes=[pltpu.VMEM((rows, width), src.dtype), pltpu.SemaphoreType.DMA(())],
        compiler_params=_params(1),
    )(idx.reshape(n // rows, 1, rows), src)


def _moe_up_kernel(xt_ref, e_ref, c_ref, ot_ref, oc_ref, first_ref, valid_ref, x_ref, wg_ref, wl_ref,
                   bg_ref, bl_ref, o_ref, wg_s, wl_s):
    i = pl.program_id(0)
    half = x_ref.shape[1]

    @pl.when(first_ref[i] == 1)
    def _():
        wg_s[...] = wg_ref[...].astype(BF16)
        wl_s[...] = wl_ref[...].astype(BF16)

    @pl.when(valid_ref[i] == 0)
    def _():
        o_ref[...] = jnp.zeros_like(o_ref)

    @pl.when(valid_ref[i] == 1)
    def _():
        words = x_ref[...]
        lo = lax.bitcast_convert_type(words << 16, F32).astype(BF16)
        hi = lax.bitcast_convert_type(words & jnp.uint32(0xFFFF0000), F32).astype(BF16)

        def proj(w_s, b_ref):
            return (jnp.dot(lo, w_s[:half, :], preferred_element_type=F32)
                    + jnp.dot(hi, w_s[half:, :], preferred_element_type=F32) + b_ref[...])

        x_glu = jnp.minimum(proj(wg_s, bg_ref), SWIGLU_LIMIT)
        x_lin = jnp.clip(proj(wl_s, bl_ref), -SWIGLU_LIMIT, SWIGLU_LIMIT)
        act = x_glu * jax.nn.sigmoid(SWIGLU_ALPHA * x_glu) * (x_lin + 1.0)
        o_ref[...] = act.astype(o_ref.dtype)


def _item_map(fn):
    return lambda i, xt, ex, ch, ot, oc, fi, va: fn(i, xt, ex, ch, ot, oc)


def _moe_up(x_sorted, w1, b1, layer, tables, *, tm, fc):
    p_rows, half = x_sorted.shape
    depth, e, d, ff2 = w1.shape
    ff = ff2 // 2
    ncb = ff // fc
    n_items = tables[0].shape[0]
    grid_spec = pltpu.PrefetchScalarGridSpec(
        num_scalar_prefetch=len(tables),
        grid=(n_items,),
        in_specs=[pl.BlockSpec((tm, half), _item_map(lambda i, xt, ex, ch, ot, oc: (xt[i], 0))),
                  pl.BlockSpec((None, None, d, fc),
                               _item_map(lambda i, xt, ex, ch, ot, oc: (layer, ex[i], 0, ch[i]))),
                  pl.BlockSpec((None, None, d, fc),
                               _item_map(lambda i, xt, ex, ch, ot, oc: (layer, ex[i], 0, ncb + ch[i]))),
                  pl.BlockSpec((None, None, 1, fc),
                               _item_map(lambda i, xt, ex, ch, ot, oc: (layer, ex[i], 0, ch[i]))),
                  pl.BlockSpec((None, None, 1, fc),
                               _item_map(lambda i, xt, ex, ch, ot, oc: (layer, ex[i], 0, ncb + ch[i])))],
        out_specs=pl.BlockSpec((tm, fc), _item_map(lambda i, xt, ex, ch, ot, oc: (ot[i], oc[i]))),
        scratch_shapes=[pltpu.VMEM((d, fc), BF16), pltpu.VMEM((d, fc), BF16)])
    b1r = b1.reshape(depth, e, 1, ff2)
    return pl.pallas_call(
        _moe_up_kernel,
        grid_spec=grid_spec,
        out_shape=jax.ShapeDtypeStruct((p_rows, ff), BF16),
        compiler_params=_params(1),
    )(*tables, x_sorted, w1, w1, b1r, b1r)


def _moe_down_kernel(xt_ref, e_ref, c_ref, ot_ref, oc_ref, first_ref, valid_ref, a_ref, w_ref, b_ref,
                     o_ref, w_s):
    i = pl.program_id(0)

    @pl.when(first_ref[i] == 1)
    def _():
        w_s[...] = w_ref[...].astype(BF16)

    @pl.when(valid_ref[i] == 0)
    def _():
        o_ref[...] = jnp.zeros_like(o_ref)

    @pl.when(valid_ref[i] == 1)
    def _():
        o_ref[...] = jnp.dot(a_ref[...], w_s[...], preferred_element_type=F32) + b_ref[...]


def _moe_down(act, w2, b2, layer, tables, *, tm, nc):
    p_rows, ff = act.shape
    depth, e, _, d = w2.shape
    n_items = tables[0].shape[0]
    grid_spec = pltpu.PrefetchScalarGridSpec(
        num_scalar_prefetch=len(tables),
        grid=(n_items,),
        in_specs=[pl.BlockSpec((tm, ff), _item_map(lambda i, xt, ex, ch, ot, oc: (xt[i], 0))),
                  pl.BlockSpec((None, None, ff, nc),
                               _item_map(lambda i, xt, ex, ch, ot, oc: (layer, ex[i], 0, ch[i]))),
                  pl.BlockSpec((None, None, 1, nc),
                               _item_map(lambda i, xt, ex, ch, ot, oc: (layer, ex[i], 0, ch[i])))],
        out_specs=pl.BlockSpec((tm, nc), _item_map(lambda i, xt, ex, ch, ot, oc: (ot[i], oc[i]))),
        scratch_shapes=[pltpu.VMEM((ff, nc), BF16)])
    return pl.pallas_call(
        _moe_down_kernel,
        grid_spec=grid_spec,
        out_shape=jax.ShapeDtypeStruct((p_rows, d), F32),
        compiler_params=_params(1),
    )(*tables, act, w2, b2.reshape(depth, e, 1, d))


def _item_tables(tiles_per_e, tile_start_e, n_tiles_max, n_chunks):
    n_items = n_tiles_max * n_chunks
    items_per_e = tiles_per_e * n_chunks
    item_end = jnp.cumsum(items_per_e)
    item_start = item_end - items_per_e
    n_valid = item_end[-1]
    n_used_tiles = n_valid // n_chunks
    i = jnp.arange(n_items, dtype=jnp.int32)
    valid = i < n_valid
    ic = jnp.minimum(i, n_valid - 1)
    ex = jnp.sum((item_end[None, :] <= ic[:, None]).astype(jnp.int32), axis=1)
    ex = jnp.minimum(ex, tiles_per_e.shape[0] - 1)
    local = ic - item_start[ex]
    per = jnp.maximum(tiles_per_e[ex], 1)
    ch = local // per
    r_local = local % per
    tile = tile_start_e[ex] + r_local
    spare = i - n_valid
    out_tile = jnp.where(valid, tile, n_used_tiles + spare // n_chunks)
    out_chunk = jnp.where(valid, ch, spare % n_chunks)
    first = valid & (r_local == 0)
    return tuple(a.astype(jnp.int32) for a in (tile, ex, ch, out_tile, out_chunk, first, valid))


def _post_kernel(pos_ref, h_ref, gates_ref, y_ref, p_ref, pn_ref, pg_ref, pp_ref, fn_ref, o_ref, ybuf,
                 sem, *, final):
    tm = h_ref.shape[0]

    def copy(j, k, row):
        return pltpu.make_async_copy(y_ref.at[pl.ds(row, 1)], ybuf.at[k, pl.ds(j, 1)], sem)

    def start(j, carry):
        for k in range(TOP_K):
            copy(j, k, pos_ref[0, 0, j * TOP_K + k]).start()
        return carry

    def wait(j, carry):
        for k in range(TOP_K):
            copy(j, k, 0).wait()
        return carry

    lax.fori_loop(0, tm, start, 0)
    lax.fori_loop(0, tm, wait, 0)

    gates = gates_ref[...]
    h2 = h_ref[...]
    for k in range(TOP_K):
        h2 = h2 + gates[:, k:k + 1] * ybuf[k]
    hn = _rms(h2, pn_ref[...]).astype(BF16)
    gate = jax.nn.sigmoid(jnp.dot(hn, pg_ref[...], preferred_element_type=F32))
    pe = jnp.dot(p_ref[...].astype(BF16), pp_ref[...], preferred_element_type=F32)
    h3 = h2 + pe * gate
    if final:
        h3 = _rms(h3, fn_ref[...])
    o_ref[...] = h3


def _post(h, pos, gates, y, p, ple_norm, ple_gate, ple_proj, final_norm, *, tm, final):
    t, d = h.shape
    pd = p.shape[1]
    const = lambda i: (0, 0)
    return pl.pallas_call(
        functools.partial(_post_kernel, final=final),
        grid=(t // tm,),
        in_specs=[pl.BlockSpec((1, 1, tm * TOP_K), lambda i: (i, 0, 0), memory_space=pltpu.SMEM),
                  pl.BlockSpec((tm, d), lambda i: (i, 0)),
                  pl.BlockSpec((tm, LANES), lambda i: (i, 0)),
                  pl.BlockSpec(memory_space=pl.ANY),
                  pl.BlockSpec((tm, pd), lambda i: (i, 0)),
                  pl.BlockSpec((1, d), const),
                  pl.BlockSpec((d, d), const),
                  pl.BlockSpec((pd, d), const),
                  pl.BlockSpec((1, d), const)],
        out_specs=pl.BlockSpec((tm, d), lambda i: (i, 0)),
        out_shape=jax.ShapeDtypeStruct((t, d), F32),
        scratch_shapes=[pltpu.VMEM((TOP_K, tm, d), F32), pltpu.SemaphoreType.DMA(())],
        compiler_params=_params(1),
    )(pos.reshape(t // tm, 1, tm * TOP_K), h, gates, y, p, ple_norm.reshape(1, d), ple_gate, ple_proj,
      final_norm.reshape(1, d))


def _moe_ple(h, p, ffn_norm, router_w, router_b, w1, b1, w2, b2, layer, ple_norm, ple_gate, ple_proj,
             final_norm, *, final, tiles):
    t, d = h.shape
    e = router_w.shape[1]
    tm_r, tm_e, fc, nc, tm_p, g_rows = tiles
    xp, ids, gates, rank, cnt = _router(h, ffn_norm, router_w, router_b, tm=tm_r)

    counts = cnt[0, :e].astype(jnp.int32)
    tiles_per_e = (counts + tm_e - 1) // tm_e
    padded = tiles_per_e * tm_e
    ends = jnp.cumsum(padded)
    starts = ends - padded
    ids4 = ids[:, :TOP_K]
    start_of = jnp.sum(jnp.where(ids4[:, :, None] == jnp.arange(e, dtype=jnp.int32), starts, 0), axis=-1)
    pos = start_of + rank[:, :TOP_K]
    n_tiles_max = (t * TOP_K) // tm_e + e
    p_rows = n_tiles_max * tm_e
    tok = jnp.broadcast_to(jnp.arange(t, dtype=jnp.int32)[:, None], (t, TOP_K))
    sorted_tok = jnp.zeros((p_rows,), jnp.int32).at[pos.reshape(-1)].set(tok.reshape(-1))
    tile_start_e = starts // tm_e

    x_sorted = _gather_rows(xp, sorted_tok, rows=g_rows)
    ff = w1.shape[3] // 2
    act = _moe_up(x_sorted, w1, b1, layer, _item_tables(tiles_per_e, tile_start_e, n_tiles_max, ff // fc),
                  tm=tm_e, fc=fc)
    y = _moe_down(act, w2, b2, layer, _item_tables(tiles_per_e, tile_start_e, n_tiles_max, d // nc),
                  tm=tm_e, nc=nc)
    return _post(h, pos, gates, y, p, ple_norm, ple_gate.astype(BF16), ple_proj.astype(BF16),
                 final_norm, tm=tm_p, final=final)


def _pick(t, pref):
    return min(pref, t)


def kernel(x, p, positions, mix_norm, ffn_norm, ev_w_in, ev_conv_w, ev_a_log, ev_dt_bias, ev_out_norm, ev_sgu_ln_g, ev_sgu_ln_b, ev_sgu_w, ev_sgu_b, ev_w_out, od_w_in, od_lambda, od_subln, od_w_out, router_w, router_b, exp_w1, exp_b1, exp_w2, exp_b2, ple_proj, ple_norm, ple_gate, final_norm):
    bsz, s, d = x.shape
    depth = mix_norm.shape[0]
    t = bsz * s
    assert bsz == 1, "sequence mixers here assume one sequence"
    h = x.reshape(t, d)
    pos1 = positions.reshape(t)

    tm = _pick(t, 512)
    tiles = (_pick(t, 512), 512, 512, 1024, _pick(t, 256), _pick(t, 512))
    rope = None

    for i in range(depth):
        j = i // 2
        if i % 2 == 0:
            w_in = ev_w_in[j]
            ab_cols = 2 * GDN_HEADS
            g_off = A_QKV + A_V + ab_cols
            w_cat = jnp.concatenate(
                [w_in[:, :A_QKV + A_V], w_in[:, g_off:],
                 jnp.pad(w_in[:, A_QKV + A_V:g_off], ((0, 0), (0, LANES - ab_cols)))], axis=1)
            proj = _norm_matmul(h, mix_norm[i], w_cat.astype(BF16), tm=tm, tn=LANES * 7, out_dtype=F32)
            gu_block = (A_QKV + A_V) // LANES
            gv_block = gu_block + B_W // LANES
            ab_block = gv_block + B_W // LANES
            out_a = _gdn(proj, ev_conv_w[j], ev_a_log[j], ev_dt_bias[j], ev_out_norm[j],
                         tb=_pick(t, 512), ab_block=ab_block)
            out_b = _sgu(proj, ev_sgu_ln_g[j], ev_sgu_ln_b[j], ev_sgu_w[j], ev_sgu_b[j],
                         tb=_pick(t, 512), gu_block=gu_block, gv_block=gv_block)
            w_out = ev_w_out[j].astype(BF16)
            h = _matmul_residual([out_a, out_b], [w_out[:A_V], w_out[A_V:]], h, tm=tm, tn=1024)
        else:
            if rope is None:
                rope = _rope_tables(pos1, tm=tm)
            lam_init = 0.8 - 0.6 * math.exp(-0.3 * i)
            qkv = _norm_matmul(h, mix_norm[i], od_w_in[j].astype(BF16), tm=tm, tn=512, out_dtype=BF16,
                               rope=(rope[0], rope[1], C_QK, 2 * C_QK, DIFF_DH ** -0.5))
            att = _diff_attn(qkv, od_lambda[j], od_subln[j], tq=_pick(t, 1024), lam_init=lam_init)
            h = _matmul_residual([att], [od_w_out[j].astype(BF16)], h, tm=tm, tn=1024)
        h = _moe_ple(h, p[i].reshape(t, -1), ffn_norm[i], router_w[i], router_b[i], exp_w1, exp_b1,
                     exp_w2, exp_b2, i, ple_norm[i], ple_gate[i], ple_proj[i], final_norm,
                     final=(i == depth - 1), tiles=tiles)
    return h.reshape(bsz, s, d)
```

```python
import functools
import math

import jax
import jax.numpy as jnp
from jax import lax
from jax.experimental import pallas as pl
from jax.experimental.pallas import tpu as pltpu

F32 = jnp.float32
BF16 = jnp.bfloat16
HI = lax.Precision.HIGHEST

NORM_EPS = 1e-6
L2_EPS = 1e-6
LN_EPS = 1e-5
LANES = 128
SUBLANES = 8
VMEM_LIMIT = 56 * 1024 * 1024

GDN_HEADS = 8
GDN_DK = 128
GDN_DV = 128
GDN_CONV = 4
GDN_CHUNK = 128
SGU_GROUPS = 8
SGU_DIM = 128
SGU_CHUNK = 128
DIFF_HEADS = 8
DIFF_DH = 128
ROPE_THETA = 10000.0
TOP_K = 4
SWIGLU_ALPHA = 1.702
SWIGLU_LIMIT = 7.0

A_QK = GDN_HEADS * GDN_DK
A_V = GDN_HEADS * GDN_DV
A_QKV = 2 * A_QK + A_V
B_W = SGU_GROUPS * SGU_DIM
C_QK = DIFF_HEADS * 2 * DIFF_DH


def _params(n_axes):
    return pltpu.CompilerParams(dimension_semantics=("arbitrary",) * n_axes,
                                vmem_limit_bytes=VMEM_LIMIT)


def _rms(x, g):
    return x * lax.rsqrt(jnp.mean(x * x, axis=-1, keepdims=True) + NORM_EPS) * g


def _lane_tile(x, reps):
    return jnp.concatenate([x] * reps, axis=1)


def _split_bf16(x):
    hi = x.astype(BF16)
    return hi, (x - hi.astype(F32)).astype(BF16)


def _dot3(a, b):
    ah, al = _split_bf16(a)
    bh, bl = _split_bf16(b)
    return (jnp.dot(ah, bh, preferred_element_type=F32)
            + (jnp.dot(ah, bl, preferred_element_type=F32) + jnp.dot(al, bh, preferred_element_type=F32)))


def _dot_exact_lhs(a_bf, b):
    b0 = b.astype(BF16)
    r = b - b0.astype(F32)
    b1 = r.astype(BF16)
    b2 = (r - b1.astype(F32)).astype(BF16)
    return (jnp.dot(a_bf, b0, preferred_element_type=F32)
            + (jnp.dot(a_bf, b1, preferred_element_type=F32) + jnp.dot(a_bf, b2, preferred_element_type=F32)))


def _gelu(x):
    return 0.5 * x * (1.0 + lax.erf(x * (2.0 ** -0.5)))


def _softplus(x):
    return jnp.maximum(x, 0.0) + jnp.log1p(jnp.exp(-jnp.abs(x)))


def _norm_matmul_kernel(x_ref, g_ref, w_ref, o_ref, xn_ref):
    @pl.when(pl.program_id(1) == 0)
    def _():
        xn_ref[...] = _rms(x_ref[...], g_ref[...]).astype(BF16)

    o_ref[...] = jnp.dot(xn_ref[...], w_ref[...], preferred_element_type=F32).astype(o_ref.dtype)


def _norm_matmul_rope_kernel(x_ref, g_ref, w_ref, cos_ref, sin_ref, o_ref, xn_ref, *, n_q_blocks,
                             n_rope_blocks, q_scale):
    j = pl.program_id(1)

    @pl.when(j == 0)
    def _():
        xn_ref[...] = _rms(x_ref[...], g_ref[...]).astype(BF16)

    y = jnp.dot(xn_ref[...], w_ref[...], preferred_element_type=F32)

    @pl.when(j < n_rope_blocks)
    def _():
        cos = cos_ref[...]
        sin = sin_ref[...]
        scale = jnp.where(j < n_q_blocks, q_scale, 1.0).astype(F32)
        for c in range(y.shape[1] // LANES):
            yc = y[:, c * LANES:(c + 1) * LANES]
            rot = pltpu.roll(yc, LANES // 2, axis=1)
            o_ref[:, c * LANES:(c + 1) * LANES] = ((yc * cos + rot * sin) * scale).astype(o_ref.dtype)

    @pl.when(j >= n_rope_blocks)
    def _():
        o_ref[...] = y.astype(o_ref.dtype)


def _norm_matmul(x, gain, w, *, tm, tn, out_dtype, rope=None):
    t, d = x.shape
    n = w.shape[1]
    grid = (t // tm, n // tn)
    in_specs = [pl.BlockSpec((tm, d), lambda i, j: (i, 0)),
                pl.BlockSpec((1, d), lambda i, j: (0, 0)),
                pl.BlockSpec((d, tn), lambda i, j: (0, j))]
    args = [x, gain.reshape(1, d), w]
    if rope is None:
        body = _norm_matmul_kernel
    else:
        cos, sin, n_q_cols, n_rope_cols, q_scale = rope
        body = functools.partial(_norm_matmul_rope_kernel, n_q_blocks=n_q_cols // tn,
                                 n_rope_blocks=n_rope_cols // tn, q_scale=q_scale)
        in_specs += [pl.BlockSpec((tm, LANES), lambda i, j: (i, 0)),
                     pl.BlockSpec((tm, LANES), lambda i, j: (i, 0))]
        args += [cos, sin]
    return pl.pallas_call(
        body,
        grid=grid,
        in_specs=in_specs,
        out_specs=pl.BlockSpec((tm, tn), lambda i, j: (i, j)),
        out_shape=jax.ShapeDtypeStruct((t, n), out_dtype),
        scratch_shapes=[pltpu.VMEM((tm, d), BF16)],
        compiler_params=_params(2),
    )(*args)


def _matmul_residual_kernel(*refs, n_acts):
    a_refs = refs[:n_acts]
    w_refs = refs[n_acts:2 * n_acts]
    res_ref = refs[2 * n_acts]
    o_ref = refs[2 * n_acts + 1]
    acc = res_ref[...]
    for a_ref, w_ref in zip(a_refs, w_refs):
        acc = acc + jnp.dot(a_ref[...], w_ref[...], preferred_element_type=F32)
    o_ref[...] = acc


def _matmul_residual(acts, ws, res, *, tm, tn):
    t, n = res.shape
    n_acts = len(acts)
    in_specs = [pl.BlockSpec((tm, a.shape[1]), lambda i, j: (i, 0)) for a in acts]
    in_specs += [pl.BlockSpec((w.shape[0], tn), lambda i, j: (0, j)) for w in ws]
    in_specs += [pl.BlockSpec((tm, tn), lambda i, j: (i, j))]
    return pl.pallas_call(
        functools.partial(_matmul_residual_kernel, n_acts=n_acts),
        grid=(t // tm, n // tn),
        in_specs=in_specs,
        out_specs=pl.BlockSpec((tm, tn), lambda i, j: (i, j)),
        out_shape=jax.ShapeDtypeStruct((t, n), F32),
        compiler_params=_params(2),
    )(*acts, *ws, res)


def _gdn_kernel(q_ref, k_ref, v_ref, z_ref, ab_ref, cwq_ref, cwk_ref, cwv_ref, alog_ref, dtb_ref,
                onorm_ref, o_ref, buf_ref, s_ref, *, tb):
    c_len = GDN_CHUNK
    h = pl.program_id(0)
    halo = SUBLANES

    @pl.when(pl.program_id(1) == 0)
    def _():
        buf_ref[:, 0:halo, :] = jnp.zeros((3, halo, LANES), F32)
        s_ref[...] = jnp.zeros_like(s_ref)

    def conv_silu(i, x_ref, cw_ref):
        x = x_ref[...]
        buf_ref[i, halo:halo + tb, :] = x
        cw = cw_ref[...]
        acc = x * cw[GDN_CONV - 1:GDN_CONV, :]
        for s in range(1, GDN_CONV):
            acc = acc + buf_ref[i, halo - s:halo - s + tb, :] * cw[GDN_CONV - 1 - s:GDN_CONV - s, :]
        buf_ref[i, 0:halo, :] = x[tb - halo:tb, :]
        return acc * jax.nn.sigmoid(acc)

    q = conv_silu(0, q_ref, cwq_ref)
    k = conv_silu(1, k_ref, cwk_ref)
    v = conv_silu(2, v_ref, cwv_ref)
    q = q * lax.rsqrt(jnp.sum(q * q, axis=-1, keepdims=True) + L2_EPS) * (GDN_DK ** -0.5)
    k = k * lax.rsqrt(jnp.sum(k * k, axis=-1, keepdims=True) + L2_EPS)

    ab = ab_ref[...]
    lane = lax.broadcasted_iota(jnp.int32, (1, LANES), 1)
    g_all = -jnp.exp(alog_ref[...]) * _softplus(ab + dtb_ref[...])
    g_col = jnp.sum(jnp.where(lane == h, g_all, 0.0), axis=1, keepdims=True)
    beta = jnp.sum(jnp.where(lane == GDN_HEADS + h, jax.nn.sigmoid(ab), 0.0), axis=1, keepdims=True)

    row = lax.broadcasted_iota(jnp.int32, (c_len, c_len), 0)
    col = lax.broadcasted_iota(jnp.int32, (c_len, c_len), 1)
    tri_incl_bf = (row >= col).astype(BF16)
    eye = (row == col).astype(F32)
    onorm = onorm_ref[...]
    nt = (((1,), (1,)), ((), ()))

    n_chunks = tb // c_len
    slices = [slice(c * c_len, (c + 1) * c_len) for c in range(n_chunks)]
    gbs = [_dot_exact_lhs(tri_incl_bf, jnp.broadcast_to(g_col[sl], (c_len, LANES))) for sl in slices]
    gams = [jnp.where(row >= col, jnp.exp(jnp.minimum(gb - gb.T, 0.0)), 0.0) for gb in gbs]
    kbs = [k[sl] * beta[sl] for sl in slices]
    scs = [lax.dot_general(jnp.concatenate([kb, q[sl]], axis=0).astype(BF16), k[sl].astype(BF16), nt,
                           preferred_element_type=F32) for kb, sl in zip(kbs, slices)]
    lows = [jnp.where(row > col, sc[:c_len] * gam, 0.0) for sc, gam in zip(scs, gams)]
    aqks = [(sc[c_len:] * gam).astype(BF16) for sc, gam in zip(scs, gams)]
    invs = [eye - low for low in lows]
    pws = lows
    for _ in range(int(math.log2(c_len)) - 1):
        pws = [_dot3(pw, pw) for pw in pws]
        invs = [inv + _dot3(inv, pw) for inv, pw in zip(invs, pws)]
    egs = [jnp.exp(gb) for gb in gbs]
    sols = [_dot3(inv, jnp.concatenate([v[sl] * beta[sl], kb * eg], axis=1))
            for inv, sl, kb, eg in zip(invs, slices, kbs, egs)]
    g_lasts = [gb[c_len - 1:c_len, :] for gb in gbs]
    wqs = [jnp.concatenate([sol[:, GDN_DV:], q[sl] * eg], axis=0).astype(BF16)
           for sol, sl, eg in zip(sols, slices, egs)]
    kdts = [(k[sl] * jnp.exp(g_last - gb)).T.astype(BF16) for sl, g_last, gb in zip(slices, g_lasts, gbs)]

    state = s_ref[...]
    for c, sl in enumerate(slices):
        ws = jnp.dot(wqs[c], state.astype(BF16), preferred_element_type=F32)
        v_new_bf = (sols[c][:, :GDN_DV] - ws[:c_len]).astype(BF16)
        o = ws[c_len:] + jnp.dot(aqks[c], v_new_bf, preferred_element_type=F32)
        state = state * jnp.exp(g_lasts[c]) + jnp.dot(kdts[c], v_new_bf, preferred_element_type=F32)
        zc = z_ref[sl, :]
        o = _rms(o, onorm) * (zc * jax.nn.sigmoid(zc))
        o_ref[sl, :] = o.astype(o_ref.dtype)
    s_ref[...] = state


def _gdn(proj, conv_w, a_log, dt_bias, out_norm, *, tb, ab_block):
    t = proj.shape[0]
    hds = GDN_HEADS

    def col(off):
        return pl.BlockSpec((tb, LANES), lambda h, b, off=off: (b, off + h))

    def cw(off):
        return pl.BlockSpec((GDN_CONV, LANES), lambda h, b, off=off: (0, off + h))

    vec = pl.BlockSpec((1, LANES), lambda h, b: (0, 0))
    pad = LANES - hds
    return pl.pallas_call(
        functools.partial(_gdn_kernel, tb=tb),
        grid=(hds, t // tb),
        in_specs=[col(0), col(hds), col(2 * hds), col(3 * hds),
                  pl.BlockSpec((tb, LANES), lambda h, b: (b, ab_block)),
                  cw(0), cw(hds), cw(2 * hds), vec, vec, vec],
        out_specs=pl.BlockSpec((tb, LANES), lambda h, b: (b, h)),
        out_shape=jax.ShapeDtypeStruct((t, A_V), BF16),
        scratch_shapes=[pltpu.VMEM((3, SUBLANES + tb, LANES), F32),
                        pltpu.VMEM((GDN_DK, GDN_DV), F32)],
        compiler_params=_params(2),
    )(proj, proj, proj, proj, proj, conv_w, conv_w, conv_w,
      jnp.pad(a_log, (0, pad)).reshape(1, LANES), jnp.pad(dt_bias, (0, pad)).reshape(1, LANES),
      out_norm.reshape(1, LANES))


def _sgu_kernel(gu_ref, gv_ref, lng_ref, lnb_ref, w_ref, b_ref, o_ref, *, tb):
    c_len = SGU_CHUNK
    row = lax.broadcasted_iota(jnp.int32, (c_len, c_len), 0)
    col = lax.broadcasted_iota(jnp.int32, (c_len, c_len), 1)
    w = jnp.where(row >= col, w_ref[...], 0.0)
    bias = b_ref[...]
    gu = _gelu(gu_ref[...])
    gv = _gelu(gv_ref[...])
    mu = jnp.mean(gv, axis=-1, keepdims=True)
    xc = gv - mu
    var = jnp.mean(xc * xc, axis=-1, keepdims=True)
    gv = xc * lax.rsqrt(var + LN_EPS) * lng_ref[...] + lnb_ref[...]
    for c in range(tb // c_len):
        sl = slice(c * c_len, (c + 1) * c_len)
        mixed = jnp.dot(w, gv[sl], precision=HI, preferred_element_type=F32) + bias
        o_ref[sl, :] = (gu[sl] * mixed).astype(o_ref.dtype)


def _sgu(proj, ln_g, ln_b, sgu_w, sgu_b, *, tb, gu_block, gv_block):
    t = proj.shape[0]
    g = SGU_GROUPS
    return pl.pallas_call(
        functools.partial(_sgu_kernel, tb=tb),
        grid=(t // tb, g),
        in_specs=[pl.BlockSpec((tb, LANES), lambda b, gi: (b, gu_block + gi)),
                  pl.BlockSpec((tb, LANES), lambda b, gi: (b, gv_block + gi)),
                  pl.BlockSpec((None, 1, SGU_DIM), lambda b, gi: (gi, 0, 0)),
                  pl.BlockSpec((None, 1, SGU_DIM), lambda b, gi: (gi, 0, 0)),
                  pl.BlockSpec((None, SGU_CHUNK, SGU_CHUNK), lambda b, gi: (gi, 0, 0)),
                  pl.BlockSpec((None, SGU_CHUNK, 1), lambda b, gi: (gi, 0, 0))],
        out_specs=pl.BlockSpec((tb, LANES), lambda b, gi: (b, gi)),
        out_shape=jax.ShapeDtypeStruct((t, B_W), BF16),
        compiler_params=_params(2),
    )(proj, proj, ln_g.reshape(g, 1, SGU_DIM), ln_b.reshape(g, 1, SGU_DIM), sgu_w,
      sgu_b.reshape(g, SGU_CHUNK, 1))


def _rope_table_kernel(pos_ref, freq_ref, cos_ref, sin_ref):
    ang = pos_ref[...].astype(F32) * freq_ref[...]
    lane = lax.broadcasted_iota(jnp.int32, ang.shape, 1)
    cos_ref[...] = jnp.cos(ang)
    sin_ref[...] = jnp.where(lane < LANES // 2, -1.0, 1.0) * jnp.sin(ang)


def _rope_tables(positions, *, tm):
    t = positions.shape[0]
    half = DIFF_DH // 2
    inv_freq = ROPE_THETA ** (-jnp.arange(half, dtype=F32) / half)
    freq = jnp.concatenate([inv_freq, inv_freq]).reshape(1, LANES)
    return pl.pallas_call(
        _rope_table_kernel,
        grid=(t // tm,),
        in_specs=[pl.BlockSpec((tm, 1), lambda i: (i, 0)),
                  pl.BlockSpec((1, LANES), lambda i: (0, 0))],
        out_specs=[pl.BlockSpec((tm, LANES), lambda i: (i, 0)),
                   pl.BlockSpec((tm, LANES), lambda i: (i, 0))],
        out_shape=[jax.ShapeDtypeStruct((t, LANES), F32), jax.ShapeDtypeStruct((t, LANES), F32)],
        compiler_params=_params(1),
    )(positions.reshape(t, 1), freq)


def _diff_attn_kernel(qi_ref, ki_ref, q_ref, k_ref, v_ref, lam_ref, subln_ref, o_ref, m_ref, l_ref,
                      acc_ref, *, lam_init, row_split):
    p = pl.program_id(1)
    qi = qi_ref[p]
    ki = ki_ref[p]
    dh = DIFF_DH
    nt = (((1,), (1,)), ((), ()))

    @pl.when(ki == 0)
    def _():
        m_ref[...] = jnp.full(m_ref.shape, -jnp.inf, F32)
        l_ref[...] = jnp.zeros_like(l_ref)
        acc_ref[...] = jnp.zeros_like(acc_ref)

    def step(masked):
        v = v_ref[...]
        tq = q_ref.shape[0]
        rows = tq // row_split
        chains = [(mm, r * rows) for mm in range(2) for r in range(row_split)]
        ss = [lax.dot_general(q_ref[r0:r0 + rows, mm * dh:(mm + 1) * dh], k_ref[:, mm * dh:(mm + 1) * dh],
                              nt, preferred_element_type=F32) for mm, r0 in chains]
        if masked:
            col = lax.broadcasted_iota(jnp.int32, ss[0].shape, 1)
            row = lax.broadcasted_iota(jnp.int32, ss[0].shape, 0)
            ss = [jnp.where(row + r0 >= col, s, -jnp.inf) for s, (mm, r0) in zip(ss, chains)]
        m_prevs = [m_ref[mm, r0:r0 + rows, :] for mm, r0 in chains]
        m_news = [jnp.maximum(m_prev, jnp.max(s, axis=1, keepdims=True)) for m_prev, s in zip(m_prevs, ss)]
        alphas = [jnp.exp(m_prev - m_new) for m_prev, m_new in zip(m_prevs, m_news)]
        prs = [jnp.exp(s - _lane_tile(m_new, s.shape[1] // LANES)) for s, m_new in zip(ss, m_news)]
        for (mm, r0), alpha, pr, m_new in zip(chains, alphas, prs, m_news):
            l_ref[mm, r0:r0 + rows, :] = alpha * l_ref[mm, r0:r0 + rows, :] + jnp.sum(pr, axis=1, keepdims=True)
            m_ref[mm, r0:r0 + rows, :] = m_new
        pvs = [jnp.dot(pr.astype(BF16), v, preferred_element_type=F32) for pr in prs]
        for (mm, r0), alpha, pv in zip(chains, alphas, pvs):
            acc_ref[mm, r0:r0 + rows, :] = (_lane_tile(alpha, acc_ref.shape[2] // LANES)
                                            * acc_ref[mm, r0:r0 + rows, :] + pv)

    @pl.when(ki < qi)
    def _():
        step(False)

    @pl.when(ki == qi)
    def _():
        step(True)
        lf = lam_ref[...]
        d01 = jnp.sum(lf[0:1] * lf[1:2], axis=1, keepdims=True)
        d23 = jnp.sum(lf[2:3] * lf[3:4], axis=1, keepdims=True)
        lam = jnp.exp(d01) - jnp.exp(d23) + lam_init
        reps = acc_ref.shape[2] // LANES
        o = (acc_ref[0] / _lane_tile(l_ref[0], reps)
             - lam * (acc_ref[1] / _lane_tile(l_ref[1], reps)))
        o = _rms(o, subln_ref[...]) * (1.0 - lam_init)
        o_ref[...] = o.astype(o_ref.dtype)


def _diff_attn(qkv, lam, subln, *, tq, lam_init, row_split=2):
    t = qkv.shape[0]
    nq = t // tq
    hw = 2 * DIFF_DH
    qi_tab = jnp.asarray([qi for qi in range(nq) for _ in range(qi + 1)], jnp.int32)
    ki_tab = jnp.asarray([ki for qi in range(nq) for ki in range(qi + 1)], jnp.int32)
    k_off = C_QK // hw
    grid_spec = pltpu.PrefetchScalarGridSpec(
        num_scalar_prefetch=2,
        grid=(DIFF_HEADS, qi_tab.shape[0]),
        in_specs=[pl.BlockSpec((tq, hw), lambda h, p, qt, kt: (qt[p], h)),
                  pl.BlockSpec((tq, hw), lambda h, p, qt, kt: (kt[p], k_off + h)),
                  pl.BlockSpec((tq, hw), lambda h, p, qt, kt: (kt[p], 2 * k_off + h)),
                  pl.BlockSpec((4, DIFF_DH), lambda h, p, qt, kt: (0, 0)),
                  pl.BlockSpec((1, hw), lambda h, p, qt, kt: (0, 0))],
        out_specs=pl.BlockSpec((tq, hw), lambda h, p, qt, kt: (qt[p], h)),
        scratch_shapes=[pltpu.VMEM((2, tq, LANES), F32), pltpu.VMEM((2, tq, LANES), F32),
                        pltpu.VMEM((2, tq, hw), F32)])
    return pl.pallas_call(
        functools.partial(_diff_attn_kernel, lam_init=lam_init, row_split=row_split),
        grid_spec=grid_spec,
        out_shape=jax.ShapeDtypeStruct((t, DIFF_HEADS * hw), BF16),
        compiler_params=_params(2),
    )(qi_tab, ki_tab, qkv, qkv, qkv, lam, subln.reshape(1, hw))


def _router_kernel(h_ref, g_ref, rw_ref, rb_ref, xp_ref, ids_ref, gates_ref, rank_ref, cnt_ref,
                   carry_ref, *, n_experts):
    tm = h_ref.shape[0]
    half = h_ref.shape[1] // 2

    @pl.when(pl.program_id(0) == 0)
    def _():
        carry_ref[...] = jnp.zeros_like(carry_ref)

    xn = _rms(h_ref[...], g_ref[...])
    bits = lax.bitcast_convert_type(xn.astype(BF16).astype(F32), jnp.uint32)
    xp_ref[...] = (bits[:, :half] >> 16) | bits[:, half:]

    logits = jnp.dot(xn, rw_ref[...], precision=HI, preferred_element_type=F32) + rb_ref[...]
    lane = lax.broadcasted_iota(jnp.int32, (tm, LANES), 1)
    lane_f = lane.astype(F32)
    cur = jnp.where(lane < n_experts, logits, -jnp.inf)
    ids, vals = [], []
    for _ in range(TOP_K):
        m = jnp.max(cur, axis=1, keepdims=True)
        idx = jnp.min(jnp.where(cur == m, lane_f, float(LANES)), axis=1, keepdims=True)
        ids.append(idx)
        vals.append(m)
        cur = jnp.where(lane_f == idx, -jnp.inf, cur)
    exps = [jnp.exp(v - vals[0]) for v in vals]
    denom = exps[0] + exps[1] + exps[2] + exps[3]
    hot = [(lane_f == idx) for idx in ids]
    multi = jnp.zeros((tm, LANES), F32)
    for hk in hot:
        multi = multi + hk.astype(F32)
    row = lax.broadcasted_iota(jnp.int32, (tm, tm), 0)
    col = lax.broadcasted_iota(jnp.int32, (tm, tm), 1)
    strict = (row > col).astype(BF16)
    before = jnp.dot(strict, multi.astype(BF16), preferred_element_type=F32) + carry_ref[...]
    ids_out = jnp.zeros((tm, LANES), F32)
    gates_out = jnp.zeros((tm, LANES), F32)
    rank_out = jnp.zeros((tm, LANES), F32)
    for kk in range(TOP_K):
        rank = jnp.sum(jnp.where(hot[kk], before, 0.0), axis=1, keepdims=True)
        ids_out = jnp.where(lane == kk, ids[kk], ids_out)
        gates_out = jnp.where(lane == kk, exps[kk] / denom, gates_out)
        rank_out = jnp.where(lane == kk, rank, rank_out)
    ids_ref[...] = ids_out.astype(jnp.int32)
    gates_ref[...] = gates_out
    rank_ref[...] = rank_out.astype(jnp.int32)
    carry_ref[...] = carry_ref[...] + jnp.sum(multi, axis=0, keepdims=True)
    cnt_ref[...] = carry_ref[...]


def _router(h, gain, router_w, router_b, *, tm):
    t, d = h.shape
    e = router_w.shape[1]
    rw = jnp.pad(router_w, ((0, 0), (0, LANES - e)))
    rb = jnp.pad(router_b, (0, LANES - e)).reshape(1, LANES)
    row_spec = pl.BlockSpec((tm, LANES), lambda i: (i, 0))
    return pl.pallas_call(
        functools.partial(_router_kernel, n_experts=e),
        grid=(t // tm,),
        in_specs=[pl.BlockSpec((tm, d), lambda i: (i, 0)),
                  pl.BlockSpec((1, d), lambda i: (0, 0)),
                  pl.BlockSpec((d, LANES), lambda i: (0, 0)),
                  pl.BlockSpec((1, LANES), lambda i: (0, 0))],
        out_specs=[pl.BlockSpec((tm, d // 2), lambda i: (i, 0)), row_spec, row_spec, row_spec,
                   pl.BlockSpec((1, LANES), lambda i: (0, 0))],
        out_shape=[jax.ShapeDtypeStruct((t, d // 2), jnp.uint32),
                   jax.ShapeDtypeStruct((t, LANES), jnp.int32),
                   jax.ShapeDtypeStruct((t, LANES), F32),
                   jax.ShapeDtypeStruct((t, LANES), jnp.int32),
                   jax.ShapeDtypeStruct((1, LANES), F32)],
        scratch_shapes=[pltpu.VMEM((1, LANES), F32)],
        compiler_params=_params(1),
    )(h, gain.reshape(1, d), rw, rb)


def _row_copy(src_ref, dst_ref, sem, src_row, dst_row):
    return pltpu.make_async_copy(src_ref.at[pl.ds(src_row, 1)], dst_ref.at[pl.ds(dst_row, 1)], sem)


DMA_UNROLL = 8


def _issue_row_gather(src_ref, dst_ref, sem, index_of, rows):
    def body(j0, carry):
        for u in range(DMA_UNROLL):
            j = j0 * DMA_UNROLL + u
            _row_copy(src_ref, dst_ref, sem, index_of(j), j).start(priority=u % 2)
        return carry

    lax.fori_loop(0, rows // DMA_UNROLL, body, 0)


def _wait_row_gather(src_ref, dst_ref, sem, rows):
    def body(j, carry):
        _row_copy(src_ref, dst_ref, sem, 0, j).wait()
        return carry

    lax.fori_loop(0, rows, body, 0, unroll=DMA_UNROLL)


def _gather_rows_kernel(idx_ref, nxt_ref, src_ref, o_ref, buf_ref, sem, *, rows):
    i = pl.program_id(0)
    slot = i % 2

    @pl.when(i == 0)
    def _():
        _issue_row_gather(src_ref, buf_ref.at[0], sem.at[0], lambda j: idx_ref[0, 0, j], rows)

    @pl.when(i + 1 < pl.num_programs(0))
    def _():
        _issue_row_gather(src_ref, buf_ref.at[1 - slot], sem.at[1 - slot], lambda j: nxt_ref[0, 0, j], rows)

    _wait_row_gather(src_ref, buf_ref.at[slot], sem.at[slot], rows)
    o_ref[...] = buf_ref[slot]


def _gather_rows(src, idx, *, rows):
    n = idx.shape[0]
    width = src.shape[1]
    steps = n // rows
    idx3 = idx.reshape(steps, 1, rows)
    return pl.pallas_call(
        functools.partial(_gather_rows_kernel, rows=rows),
        grid=(steps,),
        in_specs=[pl.BlockSpec((1, 1, rows), lambda i: (i, 0, 0), memory_space=pltpu.SMEM),
                  pl.BlockSpec((1, 1, rows), lambda i: (jnp.minimum(i + 1, steps - 1), 0, 0),
                               memory_space=pltpu.SMEM),
                  pl.BlockSpec(memory_space=pl.ANY)],
        out_specs=pl.BlockSpec((rows, width), lambda i: (i, 0)),
        out_shape=jax.ShapeDtypeStruct((n, width), src.dtype),
        scratch_shapes=[pltpu.VMEM((2, rows, width), src.dtype), pltpu.SemaphoreType.DMA((2,))],
        compiler_params=_params(1),
    )(idx3, idx3, src)


def _moe_up_kernel(xt_ref, e_ref, c_ref, ot_ref, oc_ref, first_ref, valid_ref, x_ref, wg_ref, wl_ref,
                   bg_ref, bl_ref, o_ref, wg_s, wl_s):
    i = pl.program_id(0)
    half = x_ref.shape[1]

    @pl.when(first_ref[i] == 1)
    def _():
        wg_s[...] = wg_ref[...].astype(BF16)
        wl_s[...] = wl_ref[...].astype(BF16)

    @pl.when(valid_ref[i] == 0)
    def _():
        o_ref[...] = jnp.zeros_like(o_ref)

    @pl.when(valid_ref[i] == 1)
    def _():
        words = x_ref[...]
        lo = lax.bitcast_convert_type(words << 16, F32).astype(BF16)
        hi = lax.bitcast_convert_type(words & jnp.uint32(0xFFFF0000), F32).astype(BF16)

        def proj(w_s, b_ref):
            return (jnp.dot(lo, w_s[:half, :], preferred_element_type=F32)
                    + jnp.dot(hi, w_s[half:, :], preferred_element_type=F32) + b_ref[...])

        x_glu = jnp.minimum(proj(wg_s, bg_ref), SWIGLU_LIMIT)
        x_lin = jnp.clip(proj(wl_s, bl_ref), -SWIGLU_LIMIT, SWIGLU_LIMIT)
        act = x_glu * jax.nn.sigmoid(SWIGLU_ALPHA * x_glu) * (x_lin + 1.0)
        o_ref[...] = act.astype(o_ref.dtype)


def _item_map(fn):
    return lambda i, xt, ex, ch, ot, oc, fi, va: fn(i, xt, ex, ch, ot, oc)


def _moe_up(x_sorted, w1, b1, layer, tables, *, tm, fc):
    p_rows, half = x_sorted.shape
    depth, e, d, ff2 = w1.shape
    ff = ff2 // 2
    ncb = ff // fc
    n_items = tables[0].shape[0]
    grid_spec = pltpu.PrefetchScalarGridSpec(
        num_scalar_prefetch=len(tables),
        grid=(n_items,),
        in_specs=[pl.BlockSpec((tm, half), _item_map(lambda i, xt, ex, ch, ot, oc: (xt[i], 0))),
                  pl.BlockSpec((None, None, d, fc),
                               _item_map(lambda i, xt, ex, ch, ot, oc: (layer, ex[i], 0, ch[i]))),
                  pl.BlockSpec((None, None, d, fc),
                               _item_map(lambda i, xt, ex, ch, ot, oc: (layer, ex[i], 0, ncb + ch[i]))),
                  pl.BlockSpec((None, None, 1, fc),
                               _item_map(lambda i, xt, ex, ch, ot, oc: (layer, ex[i], 0, ch[i]))),
                  pl.BlockSpec((None, None, 1, fc),
                               _item_map(lambda i, xt, ex, ch, ot, oc: (layer, ex[i], 0, ncb + ch[i])))],
        out_specs=pl.BlockSpec((tm, fc), _item_map(lambda i, xt, ex, ch, ot, oc: (ot[i], oc[i]))),
        scratch_shapes=[pltpu.VMEM((d, fc), BF16), pltpu.VMEM((d, fc), BF16)])
    b1r = b1.reshape(depth, e, 1, ff2)
    return pl.pallas_call(
        _moe_up_kernel,
        grid_spec=grid_spec,
        out_shape=jax.ShapeDtypeStruct((p_rows, ff), BF16),
        compiler_params=_params(1),
    )(*tables, x_sorted, w1, w1, b1r, b1r)


def _moe_down_kernel(xt_ref, e_ref, c_ref, ot_ref, oc_ref, first_ref, valid_ref, a_ref, w_ref, b_ref,
                     o_ref, w_s):
    i = pl.program_id(0)

    @pl.when(first_ref[i] == 1)
    def _():
        w_s[...] = w_ref[...].astype(BF16)

    @pl.when(valid_ref[i] == 0)
    def _():
        o_ref[...] = jnp.zeros_like(o_ref)

    @pl.when(valid_ref[i] == 1)
    def _():
        o_ref[...] = jnp.dot(a_ref[...], w_s[...], preferred_element_type=F32) + b_ref[...]


def _moe_down(act, w2, b2, layer, tables, *, tm, nc):
    p_rows, ff = act.shape
    depth, e, _, d = w2.shape
    n_items = tables[0].shape[0]
    grid_spec = pltpu.PrefetchScalarGridSpec(
        num_scalar_prefetch=len(tables),
        grid=(n_items,),
        in_specs=[pl.BlockSpec((tm, ff), _item_map(lambda i, xt, ex, ch, ot, oc: (xt[i], 0))),
                  pl.BlockSpec((None, None, ff, nc),
                               _item_map(lambda i, xt, ex, ch, ot, oc: (layer, ex[i], 0, ch[i]))),
                  pl.BlockSpec((None, None, 1, nc),
                               _item_map(lambda i, xt, ex, ch, ot, oc: (layer, ex[i], 0, ch[i])))],
        out_specs=pl.BlockSpec((tm, nc), _item_map(lambda i, xt, ex, ch, ot, oc: (ot[i], oc[i]))),
        scratch_shapes=[pltpu.VMEM((ff, nc), BF16)])
    return pl.pallas_call(
        _moe_down_kernel,
        grid_spec=grid_spec,
        out_shape=jax.ShapeDtypeStruct((p_rows, d), F32),
        compiler_params=_params(1),
    )(*tables, act, w2, b2.reshape(depth, e, 1, d))


def _item_tables(tiles_per_e, tile_start_e, n_tiles_max, n_chunks):
    n_items = n_tiles_max * n_chunks
    items_per_e = tiles_per_e * n_chunks
    item_end = jnp.cumsum(items_per_e)
    item_start = item_end - items_per_e
    n_valid = item_end[-1]
    n_used_tiles = n_valid // n_chunks
    i = jnp.arange(n_items, dtype=jnp.int32)
    valid = i < n_valid
    ic = jnp.minimum(i, n_valid - 1)
    ex = jnp.sum((item_end[None, :] <= ic[:, None]).astype(jnp.int32), axis=1)
    ex = jnp.minimum(ex, tiles_per_e.shape[0] - 1)
    local = ic - item_start[ex]
    per = jnp.maximum(tiles_per_e[ex], 1)
    ch = local // per
    r_local = local % per
    tile = tile_start_e[ex] + r_local
    spare = i - n_valid
    out_tile = jnp.where(valid, tile, n_used_tiles + spare // n_chunks)
    out_chunk = jnp.where(valid, ch, spare % n_chunks)
    first = valid & (r_local == 0)
    return tuple(a.astype(jnp.int32) for a in (tile, ex, ch, out_tile, out_chunk, first, valid))


def _post_kernel(pos_ref, nxt_ref, h_ref, gates_ref, y_ref, p_ref, pn_ref, pg_ref, pp_ref, fn_ref, o_ref,
                 ybuf, sem, *, final):
    tm = h_ref.shape[0]
    rows = TOP_K * tm
    i = pl.program_id(0)
    slot = i % 2

    @pl.when(i == 0)
    def _():
        _issue_row_gather(y_ref, ybuf.at[0], sem.at[0], lambda j: pos_ref[0, 0, j], rows)

    @pl.when(i + 1 < pl.num_programs(0))
    def _():
        _issue_row_gather(y_ref, ybuf.at[1 - slot], sem.at[1 - slot], lambda j: nxt_ref[0, 0, j], rows)

    _wait_row_gather(y_ref, ybuf.at[slot], sem.at[slot], rows)

    gates = gates_ref[...]
    h2 = h_ref[...]
    for k in range(TOP_K):
        h2 = h2 + gates[:, k:k + 1] * ybuf[slot, k * tm:(k + 1) * tm, :]
    hn = _rms(h2, pn_ref[...]).astype(BF16)
    gate = jax.nn.sigmoid(jnp.dot(hn, pg_ref[...], preferred_element_type=F32))
    pe = jnp.dot(p_ref[...].astype(BF16), pp_ref[...], preferred_element_type=F32)
    h3 = h2 + pe * gate
    if final:
        h3 = _rms(h3, fn_ref[...])
    o_ref[...] = h3


def _post(h, pos, gates, y, p, ple_norm, ple_gate, ple_proj, final_norm, *, tm, final):
    t, d = h.shape
    pd = p.shape[1]
    const = lambda i: (0, 0)
    steps = t // tm
    pos3 = pos.reshape(steps, tm, TOP_K).transpose(0, 2, 1).reshape(steps, 1, tm * TOP_K)
    return pl.pallas_call(
        functools.partial(_post_kernel, final=final),
        grid=(steps,),
        in_specs=[pl.BlockSpec((1, 1, tm * TOP_K), lambda i: (i, 0, 0), memory_space=pltpu.SMEM),
                  pl.BlockSpec((1, 1, tm * TOP_K), lambda i: (jnp.minimum(i + 1, steps - 1), 0, 0),
                               memory_space=pltpu.SMEM),
                  pl.BlockSpec((tm, d), lambda i: (i, 0)),
                  pl.BlockSpec((tm, LANES), lambda i: (i, 0)),
                  pl.BlockSpec(memory_space=pl.ANY),
                  pl.BlockSpec((tm, pd), lambda i: (i, 0)),
                  pl.BlockSpec((1, d), const),
                  pl.BlockSpec((d, d), const),
                  pl.BlockSpec((pd, d), const),
                  pl.BlockSpec((1, d), const)],
        out_specs=pl.BlockSpec((tm, d), lambda i: (i, 0)),
        out_shape=jax.ShapeDtypeStruct((t, d), F32),
        scratch_shapes=[pltpu.VMEM((2, TOP_K * tm, d), F32), pltpu.SemaphoreType.DMA((2,))],
        compiler_params=_params(1),
    )(pos3, pos3, h, gates, y, p, ple_norm.reshape(1, d), ple_gate, ple_proj, final_norm.reshape(1, d))


def _moe_ple(h, p, ffn_norm, router_w, router_b, w1, b1, w2, b2, layer, ple_norm, ple_gate, ple_proj,
             final_norm, *, final, tiles):
    t, d = h.shape
    e = router_w.shape[1]
    tm_r, tm_e, fc, nc, tm_p, g_rows = tiles
    xp, ids, gates, rank, cnt = _router(h, ffn_norm, router_w, router_b, tm=tm_r)

    counts = cnt[0, :e].astype(jnp.int32)
    tiles_per_e = (counts + tm_e - 1) // tm_e
    padded = tiles_per_e * tm_e
    ends = jnp.cumsum(padded)
    starts = ends - padded
    ids4 = ids[:, :TOP_K]
    start_of = jnp.sum(jnp.where(ids4[:, :, None] == jnp.arange(e, dtype=jnp.int32), starts, 0), axis=-1)
    pos = start_of + rank[:, :TOP_K]
    n_tiles_max = (t * TOP_K) // tm_e + e
    p_rows = n_tiles_max * tm_e
    tok = jnp.broadcast_to(jnp.arange(t, dtype=jnp.int32)[:, None], (t, TOP_K))
    sorted_tok = jnp.zeros((p_rows,), jnp.int32).at[pos.reshape(-1)].set(tok.reshape(-1))
    tile_start_e = starts // tm_e

    x_sorted = _gather_rows(xp, sorted_tok, rows=g_rows)
    ff = w1.shape[3] // 2
    act = _moe_up(x_sorted, w1, b1, layer, _item_tables(tiles_per_e, tile_start_e, n_tiles_max, ff // fc),
                  tm=tm_e, fc=fc)
    y = _moe_down(act, w2, b2, layer, _item_tables(tiles_per_e, tile_start_e, n_tiles_max, d // nc),
                  tm=tm_e, nc=nc)
    return _post(h, pos, gates, y, p, ple_norm, ple_gate.astype(BF16), ple_proj.astype(BF16),
                 final_norm, tm=tm_p, final=final)


def _pick(t, pref):
    return min(pref, t)


def kernel(x, p, positions, mix_norm, ffn_norm, ev_w_in, ev_conv_w, ev_a_log, ev_dt_bias, ev_out_norm, ev_sgu_ln_g, ev_sgu_ln_b, ev_sgu_w, ev_sgu_b, ev_w_out, od_w_in, od_lambda, od_subln, od_w_out, router_w, router_b, exp_w1, exp_b1, exp_w2, exp_b2, ple_proj, ple_norm, ple_gate, final_norm):
    bsz, s, d = x.shape
    depth = mix_norm.shape[0]
    t = bsz * s
    assert bsz == 1, "sequence mixers here assume one sequence"
    h = x.reshape(t, d)
    pos1 = positions.reshape(t)

    tm = _pick(t, 512)
    tm_in = _pick(t, 1024)
    tiles = (_pick(t, 512), 512, 1024, 1024, _pick(t, 256), _pick(t, 512))
    rope = None

    for i in range(depth):
        j = i // 2
        if i % 2 == 0:
            w_in = ev_w_in[j]
            ab_cols = 2 * GDN_HEADS
            g_off = A_QKV + A_V + ab_cols
            w_cat = jnp.concatenate(
                [w_in[:, :A_QKV + A_V], w_in[:, g_off:],
                 jnp.pad(w_in[:, A_QKV + A_V:g_off], ((0, 0), (0, LANES - ab_cols)))], axis=1)
            proj = _norm_matmul(h, mix_norm[i], w_cat.astype(BF16), tm=tm_in, tn=LANES * 7, out_dtype=F32)
            gu_block = (A_QKV + A_V) // LANES
            gv_block = gu_block + B_W // LANES
            ab_block = gv_block + B_W // LANES
            out_a = _gdn(proj, ev_conv_w[j], ev_a_log[j], ev_dt_bias[j], ev_out_norm[j],
                         tb=_pick(t, 1024), ab_block=ab_block)
            out_b = _sgu(proj, ev_sgu_ln_g[j], ev_sgu_ln_b[j], ev_sgu_w[j], ev_sgu_b[j],
                         tb=_pick(t, 512), gu_block=gu_block, gv_block=gv_block)
            w_out = ev_w_out[j].astype(BF16)
            h = _matmul_residual([out_a, out_b], [w_out[:A_V], w_out[A_V:]], h, tm=tm, tn=1024)
        else:
            if rope is None:
                rope = _rope_tables(pos1, tm=tm)
            lam_init = 0.8 - 0.6 * math.exp(-0.3 * i)
            qkv = _norm_matmul(h, mix_norm[i], od_w_in[j].astype(BF16), tm=tm_in, tn=1024, out_dtype=BF16,
                               rope=(rope[0], rope[1], C_QK, 2 * C_QK, DIFF_DH ** -0.5))
            att = _diff_attn(qkv, od_lambda[j], od_subln[j], tq=_pick(t, 1024), lam_init=lam_init)
            h = _matmul_residual([att], [od_w_out[j].astype(BF16)], h, tm=tm, tn=1024)
        h = _moe_ple(h, p[i].reshape(t, -1), ffn_norm[i], router_w[i], router_b[i], exp_w1, exp_b1,
                     exp_w2, exp_b2, i, ple_norm[i], ple_gate[i], ple_proj[i], final_norm,
                     final=(i == depth - 1), tiles=tiles)
    return h.reshape(bsz, s, d)
```

```python
import functools
import math

import jax
import jax.numpy as jnp
from jax import lax
from jax.experimental import pallas as pl
from jax.experimental.pallas import tpu as pltpu

F32 = jnp.float32
BF16 = jnp.bfloat16
HI = lax.Precision.HIGHEST

NORM_EPS = 1e-6
L2_EPS = 1e-6
LN_EPS = 1e-5
LANES = 128
SUBLANES = 8
VMEM_LIMIT = 56 * 1024 * 1024

GDN_HEADS = 8
GDN_DK = 128
GDN_DV = 128
GDN_CONV = 4
GDN_CHUNK = 128
SGU_GROUPS = 8
SGU_DIM = 128
SGU_CHUNK = 128
DIFF_HEADS = 8
DIFF_DH = 128
ROPE_THETA = 10000.0
TOP_K = 4
SWIGLU_ALPHA = 1.702
SWIGLU_LIMIT = 7.0

A_QK = GDN_HEADS * GDN_DK
A_V = GDN_HEADS * GDN_DV
A_QKV = 2 * A_QK + A_V
B_W = SGU_GROUPS * SGU_DIM
C_QK = DIFF_HEADS * 2 * DIFF_DH


def _params(n_axes):
    return pltpu.CompilerParams(dimension_semantics=("arbitrary",) * n_axes,
                                vmem_limit_bytes=VMEM_LIMIT)


def _rms(x, g):
    return x * lax.rsqrt(jnp.mean(x * x, axis=-1, keepdims=True) + NORM_EPS) * g


def _lane_tile(x, reps):
    return jnp.concatenate([x] * reps, axis=1)


def _split_bf16(x):
    hi = x.astype(BF16)
    return hi, (x - hi.astype(F32)).astype(BF16)


def _dot3(a, b):
    ah, al = _split_bf16(a)
    bh, bl = _split_bf16(b)
    return (jnp.dot(ah, bh, preferred_element_type=F32)
            + (jnp.dot(ah, bl, preferred_element_type=F32) + jnp.dot(al, bh, preferred_element_type=F32)))


def _dot_exact_lhs(a_bf, b):
    b0 = b.astype(BF16)
    r = b - b0.astype(F32)
    b1 = r.astype(BF16)
    b2 = (r - b1.astype(F32)).astype(BF16)
    return (jnp.dot(a_bf, b0, preferred_element_type=F32)
            + (jnp.dot(a_bf, b1, preferred_element_type=F32) + jnp.dot(a_bf, b2, preferred_element_type=F32)))


def _gelu(x):
    return 0.5 * x * (1.0 + lax.erf(x * (2.0 ** -0.5)))


def _softplus(x):
    return jnp.maximum(x, 0.0) + jnp.log1p(jnp.exp(-jnp.abs(x)))


def _norm_matmul_kernel(x_ref, g_ref, w_ref, o_ref, xn_ref):
    @pl.when(pl.program_id(1) == 0)
    def _():
        xn_ref[...] = _rms(x_ref[...], g_ref[...]).astype(BF16)

    o_ref[...] = jnp.dot(xn_ref[...], w_ref[...], preferred_element_type=F32).astype(o_ref.dtype)


def _norm_matmul_rope_kernel(x_ref, g_ref, w_ref, cos_ref, sin_ref, o_ref, xn_ref, *, n_q_blocks,
                             n_rope_blocks, q_scale):
    j = pl.program_id(1)

    @pl.when(j == 0)
    def _():
        xn_ref[...] = _rms(x_ref[...], g_ref[...]).astype(BF16)

    y = jnp.dot(xn_ref[...], w_ref[...], preferred_element_type=F32)

    @pl.when(j < n_rope_blocks)
    def _():
        cos = cos_ref[...]
        sin = sin_ref[...]
        scale = jnp.where(j < n_q_blocks, q_scale, 1.0).astype(F32)
        for c in range(y.shape[1] // LANES):
            yc = y[:, c * LANES:(c + 1) * LANES]
            rot = pltpu.roll(yc, LANES // 2, axis=1)
            o_ref[:, c * LANES:(c + 1) * LANES] = ((yc * cos + rot * sin) * scale).astype(o_ref.dtype)

    @pl.when(j >= n_rope_blocks)
    def _():
        o_ref[...] = y.astype(o_ref.dtype)


def _norm_matmul(x, gain, w, *, tm, tn, out_dtype, rope=None):
    t, d = x.shape
    n = w.shape[1]
    grid = (t // tm, n // tn)
    in_specs = [pl.BlockSpec((tm, d), lambda i, j: (i, 0)),
                pl.BlockSpec((1, d), lambda i, j: (0, 0)),
                pl.BlockSpec((d, tn), lambda i, j: (0, j))]
    args = [x, gain.reshape(1, d), w]
    if rope is None:
        body = _norm_matmul_kernel
    else:
        cos, sin, n_q_cols, n_rope_cols, q_scale = rope
        body = functools.partial(_norm_matmul_rope_kernel, n_q_blocks=n_q_cols // tn,
                                 n_rope_blocks=n_rope_cols // tn, q_scale=q_scale)
        in_specs += [pl.BlockSpec((tm, LANES), lambda i, j: (i, 0)),
                     pl.BlockSpec((tm, LANES), lambda i, j: (i, 0))]
        args += [cos, sin]
    return pl.pallas_call(
        body,
        grid=grid,
        in_specs=in_specs,
        out_specs=pl.BlockSpec((tm, tn), lambda i, j: (i, j)),
        out_shape=jax.ShapeDtypeStruct((t, n), out_dtype),
        scratch_shapes=[pltpu.VMEM((tm, d), BF16)],
        compiler_params=_params(2),
    )(*args)


def _matmul_residual_kernel(*refs, n_acts):
    a_refs = refs[:n_acts]
    w_refs = refs[n_acts:2 * n_acts]
    res_ref = refs[2 * n_acts]
    o_ref = refs[2 * n_acts + 1]
    acc = res_ref[...]
    for a_ref, w_ref in zip(a_refs, w_refs):
        acc = acc + jnp.dot(a_ref[...], w_ref[...], preferred_element_type=F32)
    o_ref[...] = acc


def _matmul_residual(acts, ws, res, *, tm, tn):
    t, n = res.shape
    n_acts = len(acts)
    in_specs = [pl.BlockSpec((tm, a.shape[1]), lambda i, j: (i, 0)) for a in acts]
    in_specs += [pl.BlockSpec((w.shape[0], tn), lambda i, j: (0, j)) for w in ws]
    in_specs += [pl.BlockSpec((tm, tn), lambda i, j: (i, j))]
    return pl.pallas_call(
        functools.partial(_matmul_residual_kernel, n_acts=n_acts),
        grid=(t // tm, n // tn),
        in_specs=in_specs,
        out_specs=pl.BlockSpec((tm, tn), lambda i, j: (i, j)),
        out_shape=jax.ShapeDtypeStruct((t, n), F32),
        compiler_params=_params(2),
    )(*acts, *ws, res)


def _gdn_kernel(q_ref, k_ref, v_ref, z_ref, ab_ref, cwq_ref, cwk_ref, cwv_ref, alog_ref, dtb_ref,
                onorm_ref, o_ref, buf_ref, s_ref, *, tb):
    c_len = GDN_CHUNK
    h = pl.program_id(0)
    halo = SUBLANES

    @pl.when(pl.program_id(1) == 0)
    def _():
        buf_ref[:, 0:halo, :] = jnp.zeros((3, halo, LANES), F32)
        s_ref[...] = jnp.zeros_like(s_ref)

    def conv_silu(i, x_ref, cw_ref):
        x = x_ref[...]
        buf_ref[i, halo:halo + tb, :] = x
        cw = cw_ref[...]
        acc = x * cw[GDN_CONV - 1:GDN_CONV, :]
        for s in range(1, GDN_CONV):
            acc = acc + buf_ref[i, halo - s:halo - s + tb, :] * cw[GDN_CONV - 1 - s:GDN_CONV - s, :]
        buf_ref[i, 0:halo, :] = x[tb - halo:tb, :]
        return acc * jax.nn.sigmoid(acc)

    q = conv_silu(0, q_ref, cwq_ref)
    k = conv_silu(1, k_ref, cwk_ref)
    v = conv_silu(2, v_ref, cwv_ref)
    q = q * lax.rsqrt(jnp.sum(q * q, axis=-1, keepdims=True) + L2_EPS) * (GDN_DK ** -0.5)
    k = k * lax.rsqrt(jnp.sum(k * k, axis=-1, keepdims=True) + L2_EPS)

    ab = ab_ref[...]
    lane = lax.broadcasted_iota(jnp.int32, (1, LANES), 1)
    g_all = -jnp.exp(alog_ref[...]) * _softplus(ab + dtb_ref[...])
    g_col = jnp.sum(jnp.where(lane == h, g_all, 0.0), axis=1, keepdims=True)
    beta = jnp.sum(jnp.where(lane == GDN_HEADS + h, jax.nn.sigmoid(ab), 0.0), axis=1, keepdims=True)

    row = lax.broadcasted_iota(jnp.int32, (c_len, c_len), 0)
    col = lax.broadcasted_iota(jnp.int32, (c_len, c_len), 1)
    tri_incl_bf = (row >= col).astype(BF16)
    eye = (row == col).astype(F32)
    onorm = onorm_ref[...]
    nt = (((1,), (1,)), ((), ()))

    n_chunks = tb // c_len
    slices = [slice(c * c_len, (c + 1) * c_len) for c in range(n_chunks)]
    gbs = [_dot_exact_lhs(tri_incl_bf, jnp.broadcast_to(g_col[sl], (c_len, LANES))) for sl in slices]
    gams = [jnp.where(row >= col, jnp.exp(jnp.minimum(gb - gb.T, 0.0)), 0.0) for gb in gbs]
    kbs = [k[sl] * beta[sl] for sl in slices]
    scs = [lax.dot_general(jnp.concatenate([kb, q[sl]], axis=0).astype(BF16), k[sl].astype(BF16), nt,
                           preferred_element_type=F32) for kb, sl in zip(kbs, slices)]
    lows = [jnp.where(row > col, sc[:c_len] * gam, 0.0) for sc, gam in zip(scs, gams)]
    aqks = [(sc[c_len:] * gam).astype(BF16) for sc, gam in zip(scs, gams)]
    invs = [eye - low for low in lows]
    pws = lows
    for _ in range(int(math.log2(c_len)) - 1):
        pws = [_dot3(pw, pw) for pw in pws]
        invs = [inv + _dot3(inv, pw) for inv, pw in zip(invs, pws)]
    egs = [jnp.exp(gb) for gb in gbs]
    sols = [_dot3(inv, jnp.concatenate([v[sl] * beta[sl], kb * eg], axis=1))
            for inv, sl, kb, eg in zip(invs, slices, kbs, egs)]
    g_lasts = [gb[c_len - 1:c_len, :] for gb in gbs]
    wqs = [jnp.concatenate([sol[:, GDN_DV:], q[sl] * eg], axis=0).astype(BF16)
           for sol, sl, eg in zip(sols, slices, egs)]
    kdts = [(k[sl] * jnp.exp(g_last - gb)).T.astype(BF16) for sl, g_last, gb in zip(slices, g_lasts, gbs)]

    state = s_ref[...]
    for c, sl in enumerate(slices):
        ws = jnp.dot(wqs[c], state.astype(BF16), preferred_element_type=F32)
        v_new_bf = (sols[c][:, :GDN_DV] - ws[:c_len]).astype(BF16)
        o = ws[c_len:] + jnp.dot(aqks[c], v_new_bf, preferred_element_type=F32)
        state = state * jnp.exp(g_lasts[c]) + jnp.dot(kdts[c], v_new_bf, preferred_element_type=F32)
        zc = z_ref[sl, :]
        o = _rms(o, onorm) * (zc * jax.nn.sigmoid(zc))
        o_ref[sl, :] = o.astype(o_ref.dtype)
    s_ref[...] = state


def _gdn(proj, conv_w, a_log, dt_bias, out_norm, *, tb, ab_block):
    t = proj.shape[0]
    hds = GDN_HEADS

    def col(off):
        return pl.BlockSpec((tb, LANES), lambda h, b, off=off: (b, off + h))

    def cw(off):
        return pl.BlockSpec((GDN_CONV, LANES), lambda h, b, off=off: (0, off + h))

    vec = pl.BlockSpec((1, LANES), lambda h, b: (0, 0))
    pad = LANES - hds
    return pl.pallas_call(
        functools.partial(_gdn_kernel, tb=tb),
        grid=(hds, t // tb),
        in_specs=[col(0), col(hds), col(2 * hds), col(3 * hds),
                  pl.BlockSpec((tb, LANES), lambda h, b: (b, ab_block)),
                  cw(0), cw(hds), cw(2 * hds), vec, vec, vec],
        out_specs=pl.BlockSpec((tb, LANES), lambda h, b: (b, h)),
        out_shape=jax.ShapeDtypeStruct((t, A_V), BF16),
        scratch_shapes=[pltpu.VMEM((3, SUBLANES + tb, LANES), F32),
                        pltpu.VMEM((GDN_DK, GDN_DV), F32)],
        compiler_params=_params(2),
    )(proj, proj, proj, proj, proj, conv_w, conv_w, conv_w,
      jnp.pad(a_log, (0, pad)).reshape(1, LANES), jnp.pad(dt_bias, (0, pad)).reshape(1, LANES),
      out_norm.reshape(1, LANES))


def _sgu_kernel(gu_ref, gv_ref, lng_ref, lnb_ref, w_ref, b_ref, o_ref, *, tb):
    c_len = SGU_CHUNK
    row = lax.broadcasted_iota(jnp.int32, (c_len, c_len), 0)
    col = lax.broadcasted_iota(jnp.int32, (c_len, c_len), 1)
    w = jnp.where(row >= col, w_ref[...], 0.0)
    bias = b_ref[...]
    gu = _gelu(gu_ref[...])
    gv = _gelu(gv_ref[...])
    mu = jnp.mean(gv, axis=-1, keepdims=True)
    xc = gv - mu
    var = jnp.mean(xc * xc, axis=-1, keepdims=True)
    gv = xc * lax.rsqrt(var + LN_EPS) * lng_ref[...] + lnb_ref[...]
    for c in range(tb // c_len):
        sl = slice(c * c_len, (c + 1) * c_len)
        mixed = jnp.dot(w, gv[sl], precision=HI, preferred_element_type=F32) + bias
        o_ref[sl, :] = (gu[sl] * mixed).astype(o_ref.dtype)


def _sgu(proj, ln_g, ln_b, sgu_w, sgu_b, *, tb, gu_block, gv_block):
    t = proj.shape[0]
    g = SGU_GROUPS
    return pl.pallas_call(
        functools.partial(_sgu_kernel, tb=tb),
        grid=(t // tb, g),
        in_specs=[pl.BlockSpec((tb, LANES), lambda b, gi: (b, gu_block + gi)),
                  pl.BlockSpec((tb, LANES), lambda b, gi: (b, gv_block + gi)),
                  pl.BlockSpec((None, 1, SGU_DIM), lambda b, gi: (gi, 0, 0)),
                  pl.BlockSpec((None, 1, SGU_DIM), lambda b, gi: (gi, 0, 0)),
                  pl.BlockSpec((None, SGU_CHUNK, SGU_CHUNK), lambda b, gi: (gi, 0, 0)),
                  pl.BlockSpec((None, SGU_CHUNK, 1), lambda b, gi: (gi, 0, 0))],
        out_specs=pl.BlockSpec((tb, LANES), lambda b, gi: (b, gi)),
        out_shape=jax.ShapeDtypeStruct((t, B_W), BF16),
        compiler_params=_params(2),
    )(proj, proj, ln_g.reshape(g, 1, SGU_DIM), ln_b.reshape(g, 1, SGU_DIM), sgu_w,
      sgu_b.reshape(g, SGU_CHUNK, 1))


def _rope_table_kernel(pos_ref, freq_ref, cos_ref, sin_ref):
    ang = pos_ref[...].astype(F32) * freq_ref[...]
    lane = lax.broadcasted_iota(jnp.int32, ang.shape, 1)
    cos_ref[...] = jnp.cos(ang)
    sin_ref[...] = jnp.where(lane < LANES // 2, -1.0, 1.0) * jnp.sin(ang)


def _rope_tables(positions, *, tm):
    t = positions.shape[0]
    half = DIFF_DH // 2
    inv_freq = ROPE_THETA ** (-jnp.arange(half, dtype=F32) / half)
    freq = jnp.concatenate([inv_freq, inv_freq]).reshape(1, LANES)
    return pl.pallas_call(
        _rope_table_kernel,
        grid=(t // tm,),
        in_specs=[pl.BlockSpec((tm, 1), lambda i: (i, 0)),
                  pl.BlockSpec((1, LANES), lambda i: (0, 0))],
        out_specs=[pl.BlockSpec((tm, LANES), lambda i: (i, 0)),
                   pl.BlockSpec((tm, LANES), lambda i: (i, 0))],
        out_shape=[jax.ShapeDtypeStruct((t, LANES), F32), jax.ShapeDtypeStruct((t, LANES), F32)],
        compiler_params=_params(1),
    )(positions.reshape(t, 1), freq)


def _diff_attn_kernel(qi_ref, ki_ref, q_ref, k_ref, v_ref, lam_ref, subln_ref, o_ref, m_ref, l_ref,
                      acc_ref, *, lam_init, row_split):
    p = pl.program_id(1)
    qi = qi_ref[p]
    ki = ki_ref[p]
    dh = DIFF_DH
    nt = (((1,), (1,)), ((), ()))

    @pl.when(ki == 0)
    def _():
        m_ref[...] = jnp.full(m_ref.shape, -jnp.inf, F32)
        l_ref[...] = jnp.zeros_like(l_ref)
        acc_ref[...] = jnp.zeros_like(acc_ref)

    def step(masked):
        v = v_ref[...]
        tq = q_ref.shape[0]
        rows = tq // row_split
        chains = [(mm, r * rows) for mm in range(2) for r in range(row_split)]
        ss = [lax.dot_general(q_ref[r0:r0 + rows, mm * dh:(mm + 1) * dh], k_ref[:, mm * dh:(mm + 1) * dh],
                              nt, preferred_element_type=F32) for mm, r0 in chains]
        if masked:
            col = lax.broadcasted_iota(jnp.int32, ss[0].shape, 1)
            row = lax.broadcasted_iota(jnp.int32, ss[0].shape, 0)
            ss = [jnp.where(row + r0 >= col, s, -jnp.inf) for s, (mm, r0) in zip(ss, chains)]
        m_prevs = [m_ref[mm, r0:r0 + rows, :] for mm, r0 in chains]
        m_news = [jnp.maximum(m_prev, jnp.max(s, axis=1, keepdims=True)) for m_prev, s in zip(m_prevs, ss)]
        alphas = [jnp.exp(m_prev - m_new) for m_prev, m_new in zip(m_prevs, m_news)]
        prs = [jnp.exp(s - _lane_tile(m_new, s.shape[1] // LANES)) for s, m_new in zip(ss, m_news)]
        for (mm, r0), alpha, pr, m_new in zip(chains, alphas, prs, m_news):
            l_ref[mm, r0:r0 + rows, :] = alpha * l_ref[mm, r0:r0 + rows, :] + jnp.sum(pr, axis=1, keepdims=True)
            m_ref[mm, r0:r0 + rows, :] = m_new
        pvs = [jnp.dot(pr.astype(BF16), v, preferred_element_type=F32) for pr in prs]
        for (mm, r0), alpha, pv in zip(chains, alphas, pvs):
            acc_ref[mm, r0:r0 + rows, :] = (_lane_tile(alpha, acc_ref.shape[2] // LANES)
                                            * acc_ref[mm, r0:r0 + rows, :] + pv)

    @pl.when(ki < qi)
    def _():
        step(False)

    @pl.when(ki == qi)
    def _():
        step(True)
        lf = lam_ref[...]
        d01 = jnp.sum(lf[0:1] * lf[1:2], axis=1, keepdims=True)
        d23 = jnp.sum(lf[2:3] * lf[3:4], axis=1, keepdims=True)
        lam = jnp.exp(d01) - jnp.exp(d23) + lam_init
        reps = acc_ref.shape[2] // LANES
        o = (acc_ref[0] / _lane_tile(l_ref[0], reps)
             - lam * (acc_ref[1] / _lane_tile(l_ref[1], reps)))
        o = _rms(o, subln_ref[...]) * (1.0 - lam_init)
        o_ref[...] = o.astype(o_ref.dtype)


def _diff_attn(qkv, lam, subln, *, tq, lam_init, row_split=2):
    t = qkv.shape[0]
    nq = t // tq
    hw = 2 * DIFF_DH
    qi_tab = jnp.asarray([qi for qi in range(nq) for _ in range(qi + 1)], jnp.int32)
    ki_tab = jnp.asarray([ki for qi in range(nq) for ki in range(qi + 1)], jnp.int32)
    k_off = C_QK // hw
    grid_spec = pltpu.PrefetchScalarGridSpec(
        num_scalar_prefetch=2,
        grid=(DIFF_HEADS, qi_tab.shape[0]),
        in_specs=[pl.BlockSpec((tq, hw), lambda h, p, qt, kt: (qt[p], h)),
                  pl.BlockSpec((tq, hw), lambda h, p, qt, kt: (kt[p], k_off + h)),
                  pl.BlockSpec((tq, hw), lambda h, p, qt, kt: (kt[p], 2 * k_off + h)),
                  pl.BlockSpec((4, DIFF_DH), lambda h, p, qt, kt: (0, 0)),
                  pl.BlockSpec((1, hw), lambda h, p, qt, kt: (0, 0))],
        out_specs=pl.BlockSpec((tq, hw), lambda h, p, qt, kt: (qt[p], h)),
        scratch_shapes=[pltpu.VMEM((2, tq, LANES), F32), pltpu.VMEM((2, tq, LANES), F32),
                        pltpu.VMEM((2, tq, hw), F32)])
    return pl.pallas_call(
        functools.partial(_diff_attn_kernel, lam_init=lam_init, row_split=row_split),
        grid_spec=grid_spec,
        out_shape=jax.ShapeDtypeStruct((t, DIFF_HEADS * hw), BF16),
        compiler_params=_params(2),
    )(qi_tab, ki_tab, qkv, qkv, qkv, lam, subln.reshape(1, hw))


def _router_kernel(h_ref, g_ref, rw_ref, rb_ref, xp_ref, ids_ref, gates_ref, rank_ref, cnt_ref,
                   carry_ref, *, n_experts):
    tm = h_ref.shape[0]
    half = h_ref.shape[1] // 2

    @pl.when(pl.program_id(0) == 0)
    def _():
        carry_ref[...] = jnp.zeros_like(carry_ref)

    xn = _rms(h_ref[...], g_ref[...])
    bits = lax.bitcast_convert_type(xn.astype(BF16).astype(F32), jnp.uint32)
    xp_ref[...] = (bits[:, :half] >> 16) | bits[:, half:]

    logits = jnp.dot(xn, rw_ref[...], precision=HI, preferred_element_type=F32) + rb_ref[...]
    lane = lax.broadcasted_iota(jnp.int32, (tm, LANES), 1)
    lane_f = lane.astype(F32)
    cur = jnp.where(lane < n_experts, logits, -jnp.inf)
    ids, vals = [], []
    for _ in range(TOP_K):
        m = jnp.max(cur, axis=1, keepdims=True)
        idx = jnp.min(jnp.where(cur == m, lane_f, float(LANES)), axis=1, keepdims=True)
        ids.append(idx)
        vals.append(m)
        cur = jnp.where(lane_f == idx, -jnp.inf, cur)
    exps = [jnp.exp(v - vals[0]) for v in vals]
    denom = exps[0] + exps[1] + exps[2] + exps[3]
    hot = [(lane_f == idx) for idx in ids]
    multi = jnp.zeros((tm, LANES), F32)
    for hk in hot:
        multi = multi + hk.astype(F32)
    row = lax.broadcasted_iota(jnp.int32, (tm, tm), 0)
    col = lax.broadcasted_iota(jnp.int32, (tm, tm), 1)
    strict = (row > col).astype(BF16)
    before = jnp.dot(strict, multi.astype(BF16), preferred_element_type=F32) + carry_ref[...]
    ids_out = jnp.zeros((tm, LANES), F32)
    gates_out = jnp.zeros((tm, LANES), F32)
    rank_out = jnp.zeros((tm, LANES), F32)
    for kk in range(TOP_K):
        rank = jnp.sum(jnp.where(hot[kk], before, 0.0), axis=1, keepdims=True)
        ids_out = jnp.where(lane == kk, ids[kk], ids_out)
        gates_out = jnp.where(lane == kk, exps[kk] / denom, gates_out)
        rank_out = jnp.where(lane == kk, rank, rank_out)
    ids_ref[...] = ids_out.astype(jnp.int32)
    gates_ref[...] = gates_out
    rank_ref[...] = rank_out.astype(jnp.int32)
    carry_ref[...] = carry_ref[...] + jnp.sum(multi, axis=0, keepdims=True)
    cnt_ref[...] = carry_ref[...]


def _router(h, gain, router_w, router_b, *, tm):
    t, d = h.shape
    e = router_w.shape[1]
    rw = jnp.pad(router_w, ((0, 0), (0, LANES - e)))
    rb = jnp.pad(router_b, (0, LANES - e)).reshape(1, LANES)
    row_spec = pl.BlockSpec((tm, LANES), lambda i: (i, 0))
    return pl.pallas_call(
        functools.partial(_router_kernel, n_experts=e),
        grid=(t // tm,),
        in_specs=[pl.BlockSpec((tm, d), lambda i: (i, 0)),
                  pl.BlockSpec((1, d), lambda i: (0, 0)),
                  pl.BlockSpec((d, LANES), lambda i: (0, 0)),
                  pl.BlockSpec((1, LANES), lambda i: (0, 0))],
        out_specs=[pl.BlockSpec((tm, d // 2), lambda i: (i, 0)), row_spec, row_spec, row_spec,
                   pl.BlockSpec((1, LANES), lambda i: (0, 0))],
        out_shape=[jax.ShapeDtypeStruct((t, d // 2), jnp.uint32),
                   jax.ShapeDtypeStruct((t, LANES), jnp.int32),
                   jax.ShapeDtypeStruct((t, LANES), F32),
                   jax.ShapeDtypeStruct((t, LANES), jnp.int32),
                   jax.ShapeDtypeStruct((1, LANES), F32)],
        scratch_shapes=[pltpu.VMEM((1, LANES), F32)],
        compiler_params=_params(1),
    )(h, gain.reshape(1, d), rw, rb)


def _row_copy(src_ref, dst_ref, sem, src_row, dst_row):
    return pltpu.make_async_copy(src_ref.at[pl.ds(src_row, 1)], dst_ref.at[pl.ds(dst_row, 1)], sem)


DMA_UNROLL = 8


def _issue_row_gather(src_ref, dst_ref, sem, index_of, rows, skip_negative=False):
    def body(j0, carry):
        for u in range(DMA_UNROLL):
            j = j0 * DMA_UNROLL + u
            idx = index_of(j)
            if skip_negative:
                @pl.when(idx >= 0)
                def _():
                    _row_copy(src_ref, dst_ref, sem, idx, j).start(priority=u % 2)
            else:
                _row_copy(src_ref, dst_ref, sem, idx, j).start(priority=u % 2)
        return carry

    lax.fori_loop(0, rows // DMA_UNROLL, body, 0)


def _wait_row_gather(src_ref, dst_ref, sem, rows, index_of=None):
    def body(j, carry):
        if index_of is None:
            _row_copy(src_ref, dst_ref, sem, 0, j).wait()
        else:
            @pl.when(index_of(j) >= 0)
            def _():
                _row_copy(src_ref, dst_ref, sem, 0, j).wait()
        return carry

    lax.fori_loop(0, rows, body, 0, unroll=DMA_UNROLL)


def _gather_rows_kernel(idx_ref, nxt_ref, src_ref, o_ref, buf_ref, sem, *, rows):
    i = pl.program_id(0)
    slot = i % 2

    @pl.when(i == 0)
    def _():
        buf_ref[...] = jnp.zeros_like(buf_ref)
        _issue_row_gather(src_ref, buf_ref.at[0], sem.at[0], lambda j: idx_ref[0, 0, j], rows, True)

    @pl.when(i + 1 < pl.num_programs(0))
    def _():
        _issue_row_gather(src_ref, buf_ref.at[1 - slot], sem.at[1 - slot], lambda j: nxt_ref[0, 0, j], rows,
                          True)

    _wait_row_gather(src_ref, buf_ref.at[slot], sem.at[slot], rows, lambda j: idx_ref[0, 0, j])
    o_ref[...] = buf_ref[slot]


def _gather_rows(src, idx, *, rows):
    n = idx.shape[0]
    width = src.shape[1]
    steps = n // rows
    idx3 = idx.reshape(steps, 1, rows)
    return pl.pallas_call(
        functools.partial(_gather_rows_kernel, rows=rows),
        grid=(steps,),
        in_specs=[pl.BlockSpec((1, 1, rows), lambda i: (i, 0, 0), memory_space=pltpu.SMEM),
                  pl.BlockSpec((1, 1, rows), lambda i: (jnp.minimum(i + 1, steps - 1), 0, 0),
                               memory_space=pltpu.SMEM),
                  pl.BlockSpec(memory_space=pl.ANY)],
        out_specs=pl.BlockSpec((rows, width), lambda i: (i, 0)),
        out_shape=jax.ShapeDtypeStruct((n, width), src.dtype),
        scratch_shapes=[pltpu.VMEM((2, rows, width), src.dtype), pltpu.SemaphoreType.DMA((2,))],
        compiler_params=_params(1),
    )(idx3, idx3, src)


ITEM_SPARE, ITEM_FULL, ITEM_HALF = 0, 1, 2


def _moe_up_kernel(xt_ref, e_ref, c_ref, ot_ref, oc_ref, first_ref, mode_ref, x_ref, wg_ref, wl_ref,
                   bg_ref, bl_ref, o_ref, wg_s, wl_s):
    i = pl.program_id(0)
    tm = x_ref.shape[0]
    half = x_ref.shape[1]

    @pl.when(first_ref[i] == 1)
    def _():
        wg_s[...] = wg_ref[...].astype(BF16)
        wl_s[...] = wl_ref[...].astype(BF16)

    @pl.when(mode_ref[i] == ITEM_SPARE)
    def _():
        o_ref[...] = jnp.zeros_like(o_ref)

    def compute(n_rows):
        words = x_ref[:n_rows, :]
        lo = lax.bitcast_convert_type(words << 16, F32).astype(BF16)
        hi = lax.bitcast_convert_type(words & jnp.uint32(0xFFFF0000), F32).astype(BF16)

        def proj(w_s, b_ref):
            return (jnp.dot(lo, w_s[:half, :], preferred_element_type=F32)
                    + jnp.dot(hi, w_s[half:, :], preferred_element_type=F32) + b_ref[...])

        x_glu = jnp.minimum(proj(wg_s, bg_ref), SWIGLU_LIMIT)
        x_lin = jnp.clip(proj(wl_s, bl_ref), -SWIGLU_LIMIT, SWIGLU_LIMIT)
        act = x_glu * jax.nn.sigmoid(SWIGLU_ALPHA * x_glu) * (x_lin + 1.0)
        o_ref[:n_rows, :] = act.astype(o_ref.dtype)

    @pl.when(mode_ref[i] == ITEM_FULL)
    def _():
        compute(tm)

    @pl.when(mode_ref[i] == ITEM_HALF)
    def _():
        compute(tm // 2)
        o_ref[tm // 2:, :] = jnp.zeros((tm - tm // 2, o_ref.shape[1]), o_ref.dtype)


def _item_map(fn):
    return lambda i, xt, ex, ch, ot, oc, fi, mo: fn(i, xt, ex, ch, ot, oc)


def _moe_up(x_sorted, w1, b1, layer, tables, *, tm, fc):
    p_rows, half = x_sorted.shape
    depth, e, d, ff2 = w1.shape
    ff = ff2 // 2
    ncb = ff // fc
    n_items = tables[0].shape[0]
    grid_spec = pltpu.PrefetchScalarGridSpec(
        num_scalar_prefetch=len(tables),
        grid=(n_items,),
        in_specs=[pl.BlockSpec((tm, half), _item_map(lambda i, xt, ex, ch, ot, oc: (xt[i], 0))),
                  pl.BlockSpec((None, None, d, fc),
                               _item_map(lambda i, xt, ex, ch, ot, oc: (layer, ex[i], 0, ch[i]))),
                  pl.BlockSpec((None, None, d, fc),
                               _item_map(lambda i, xt, ex, ch, ot, oc: (layer, ex[i], 0, ncb + ch[i]))),
                  pl.BlockSpec((None, None, 1, fc),
                               _item_map(lambda i, xt, ex, ch, ot, oc: (layer, ex[i], 0, ch[i]))),
                  pl.BlockSpec((None, None, 1, fc),
                               _item_map(lambda i, xt, ex, ch, ot, oc: (layer, ex[i], 0, ncb + ch[i])))],
        out_specs=pl.BlockSpec((tm, fc), _item_map(lambda i, xt, ex, ch, ot, oc: (ot[i], oc[i]))),
        scratch_shapes=[pltpu.VMEM((d, fc), BF16), pltpu.VMEM((d, fc), BF16)])
    b1r = b1.reshape(depth, e, 1, ff2)
    return pl.pallas_call(
        _moe_up_kernel,
        grid_spec=grid_spec,
        out_shape=jax.ShapeDtypeStruct((p_rows, ff), BF16),
        compiler_params=_params(1),
    )(*tables, x_sorted, w1, w1, b1r, b1r)


def _moe_down_kernel(xt_ref, e_ref, c_ref, ot_ref, oc_ref, first_ref, mode_ref, a_ref, w_ref, b_ref,
                     o_ref, w_s):
    i = pl.program_id(0)
    tm = a_ref.shape[0]

    @pl.when(first_ref[i] == 1)
    def _():
        w_s[...] = w_ref[...].astype(BF16)

    @pl.when(mode_ref[i] == ITEM_SPARE)
    def _():
        o_ref[...] = jnp.zeros_like(o_ref)

    @pl.when(mode_ref[i] == ITEM_FULL)
    def _():
        o_ref[...] = jnp.dot(a_ref[...], w_s[...], preferred_element_type=F32) + b_ref[...]

    @pl.when(mode_ref[i] == ITEM_HALF)
    def _():
        o_ref[:tm // 2, :] = (jnp.dot(a_ref[:tm // 2, :], w_s[...], preferred_element_type=F32)
                              + b_ref[...])
        o_ref[tm // 2:, :] = jnp.zeros((tm - tm // 2, o_ref.shape[1]), o_ref.dtype)


def _moe_down(act, w2, b2, layer, tables, *, tm, nc):
    p_rows, ff = act.shape
    depth, e, _, d = w2.shape
    n_items = tables[0].shape[0]
    grid_spec = pltpu.PrefetchScalarGridSpec(
        num_scalar_prefetch=len(tables),
        grid=(n_items,),
        in_specs=[pl.BlockSpec((tm, ff), _item_map(lambda i, xt, ex, ch, ot, oc: (xt[i], 0))),
                  pl.BlockSpec((None, None, ff, nc),
                               _item_map(lambda i, xt, ex, ch, ot, oc: (layer, ex[i], 0, ch[i]))),
                  pl.BlockSpec((None, None, 1, nc),
                               _item_map(lambda i, xt, ex, ch, ot, oc: (layer, ex[i], 0, ch[i])))],
        out_specs=pl.BlockSpec((tm, nc), _item_map(lambda i, xt, ex, ch, ot, oc: (ot[i], oc[i]))),
        scratch_shapes=[pltpu.VMEM((ff, nc), BF16)])
    return pl.pallas_call(
        _moe_down_kernel,
        grid_spec=grid_spec,
        out_shape=jax.ShapeDtypeStruct((p_rows, d), F32),
        compiler_params=_params(1),
    )(*tables, act, w2, b2.reshape(depth, e, 1, d))


def _item_tables(counts, tm_e, tile_start_e, n_tiles_max, n_chunks):
    tiles_per_e = (counts + tm_e - 1) // tm_e
    n_items = n_tiles_max * n_chunks
    items_per_e = tiles_per_e * n_chunks
    item_end = jnp.cumsum(items_per_e)
    item_start = item_end - items_per_e
    n_valid = item_end[-1]
    n_used_tiles = n_valid // n_chunks
    i = jnp.arange(n_items, dtype=jnp.int32)
    valid = i < n_valid
    ic = jnp.minimum(i, n_valid - 1)
    ex = jnp.sum((item_end[None, :] <= ic[:, None]).astype(jnp.int32), axis=1)
    ex = jnp.minimum(ex, tiles_per_e.shape[0] - 1)
    local = ic - item_start[ex]
    per = jnp.maximum(tiles_per_e[ex], 1)
    ch = local // per
    r_local = local % per
    tile = tile_start_e[ex] + r_local
    spare = i - n_valid
    out_tile = jnp.where(valid, tile, n_used_tiles + spare // n_chunks)
    out_chunk = jnp.where(valid, ch, spare % n_chunks)
    first = valid & (r_local == 0)
    last_rows = counts[ex] - (per - 1) * tm_e
    is_half = (r_local == per - 1) & (last_rows <= tm_e // 2)
    mode = jnp.where(valid, jnp.where(is_half, ITEM_HALF, ITEM_FULL), ITEM_SPARE)
    return tuple(a.astype(jnp.int32) for a in (tile, ex, ch, out_tile, out_chunk, first, mode))


def _post_kernel(pos_ref, nxt_ref, h_ref, gates_ref, y_ref, p_ref, pn_ref, pg_ref, pp_ref, fn_ref, o_ref,
                 ybuf, sem, *, final):
    tm = h_ref.shape[0]
    rows = TOP_K * tm
    i = pl.program_id(0)
    slot = i % 2

    @pl.when(i == 0)
    def _():
        _issue_row_gather(y_ref, ybuf.at[0], sem.at[0], lambda j: pos_ref[0, 0, j], rows)

    @pl.when(i + 1 < pl.num_programs(0))
    def _():
        _issue_row_gather(y_ref, ybuf.at[1 - slot], sem.at[1 - slot], lambda j: nxt_ref[0, 0, j], rows)

    _wait_row_gather(y_ref, ybuf.at[slot], sem.at[slot], rows)

    gates = gates_ref[...]
    h2 = h_ref[...]
    for k in range(TOP_K):
        h2 = h2 + gates[:, k:k + 1] * ybuf[slot, k * tm:(k + 1) * tm, :]
    hn = _rms(h2, pn_ref[...]).astype(BF16)
    gate = jax.nn.sigmoid(jnp.dot(hn, pg_ref[...], preferred_element_type=F32))
    pe = jnp.dot(p_ref[...].astype(BF16), pp_ref[...], preferred_element_type=F32)
    h3 = h2 + pe * gate
    if final:
        h3 = _rms(h3, fn_ref[...])
    o_ref[...] = h3


def _post(h, pos, gates, y, p, ple_norm, ple_gate, ple_proj, final_norm, *, tm, final):
    t, d = h.shape
    pd = p.shape[1]
    const = lambda i: (0, 0)
    steps = t // tm
    pos3 = pos.reshape(steps, tm, TOP_K).transpose(0, 2, 1).reshape(steps, 1, tm * TOP_K)
    return pl.pallas_call(
        functools.partial(_post_kernel, final=final),
        grid=(steps,),
        in_specs=[pl.BlockSpec((1, 1, tm * TOP_K), lambda i: (i, 0, 0), memory_space=pltpu.SMEM),
                  pl.BlockSpec((1, 1, tm * TOP_K), lambda i: (jnp.minimum(i + 1, steps - 1), 0, 0),
                               memory_space=pltpu.SMEM),
                  pl.BlockSpec((tm, d), lambda i: (i, 0)),
                  pl.BlockSpec((tm, LANES), lambda i: (i, 0)),
                  pl.BlockSpec(memory_space=pl.ANY),
                  pl.BlockSpec((tm, pd), lambda i: (i, 0)),
                  pl.BlockSpec((1, d), const),
                  pl.BlockSpec((d, d), const),
                  pl.BlockSpec((pd, d), const),
                  pl.BlockSpec((1, d), const)],
        out_specs=pl.BlockSpec((tm, d), lambda i: (i, 0)),
        out_shape=jax.ShapeDtypeStruct((t, d), F32),
        scratch_shapes=[pltpu.VMEM((2, TOP_K * tm, d), F32), pltpu.SemaphoreType.DMA((2,))],
        compiler_params=_params(1),
    )(pos3, pos3, h, gates, y, p, ple_norm.reshape(1, d), ple_gate, ple_proj, final_norm.reshape(1, d))


def _moe_ple(h, p, ffn_norm, router_w, router_b, w1, b1, w2, b2, layer, ple_norm, ple_gate, ple_proj,
             final_norm, *, final, tiles):
    t, d = h.shape
    e = router_w.shape[1]
    tm_r, tm_e, fc, nc, tm_p, g_rows = tiles
    xp, ids, gates, rank, cnt = _router(h, ffn_norm, router_w, router_b, tm=tm_r)

    counts = cnt[0, :e].astype(jnp.int32)
    tiles_per_e = (counts + tm_e - 1) // tm_e
    padded = tiles_per_e * tm_e
    ends = jnp.cumsum(padded)
    starts = ends - padded
    ids4 = ids[:, :TOP_K]
    start_of = jnp.sum(jnp.where(ids4[:, :, None] == jnp.arange(e, dtype=jnp.int32), starts, 0), axis=-1)
    pos = start_of + rank[:, :TOP_K]
    n_tiles_max = (t * TOP_K) // tm_e + e
    p_rows = n_tiles_max * tm_e
    tok = jnp.broadcast_to(jnp.arange(t, dtype=jnp.int32)[:, None], (t, TOP_K))
    sorted_tok = jnp.full((p_rows,), -1, jnp.int32).at[pos.reshape(-1)].set(tok.reshape(-1))
    tile_start_e = starts // tm_e

    x_sorted = _gather_rows(xp, sorted_tok, rows=g_rows)
    ff = w1.shape[3] // 2
    act = _moe_up(x_sorted, w1, b1, layer, _item_tables(counts, tm_e, tile_start_e, n_tiles_max, ff // fc),
                  tm=tm_e, fc=fc)
    y = _moe_down(act, w2, b2, layer, _item_tables(counts, tm_e, tile_start_e, n_tiles_max, d // nc),
                  tm=tm_e, nc=nc)
    return _post(h, pos, gates, y, p, ple_norm, ple_gate.astype(BF16), ple_proj.astype(BF16),
                 final_norm, tm=tm_p, final=final)


def _pick(t, pref):
    return min(pref, t)


def kernel(x, p, positions, mix_norm, ffn_norm, ev_w_in, ev_conv_w, ev_a_log, ev_dt_bias, ev_out_norm, ev_sgu_ln_g, ev_sgu_ln_b, ev_sgu_w, ev_sgu_b, ev_w_out, od_w_in, od_lambda, od_subln, od_w_out, router_w, router_b, exp_w1, exp_b1, exp_w2, exp_b2, ple_proj, ple_norm, ple_gate, final_norm):
    bsz, s, d = x.shape
    depth = mix_norm.shape[0]
    t = bsz * s
    assert bsz == 1, "sequence mixers here assume one sequence"
    h = x.reshape(t, d)
    pos1 = positions.reshape(t)

    tm = _pick(t, 512)
    tm_in = _pick(t, 1024)
    tiles = (_pick(t, 512), 512, 1024, 1024, _pick(t, 256), _pick(t, 512))
    rope = None

    for i in range(depth):
        j = i // 2
        if i % 2 == 0:
            w_in = ev_w_in[j]
            ab_cols = 2 * GDN_HEADS
            g_off = A_QKV + A_V + ab_cols
            w_cat = jnp.concatenate(
                [w_in[:, :A_QKV + A_V], w_in[:, g_off:],
                 jnp.pad(w_in[:, A_QKV + A_V:g_off], ((0, 0), (0, LANES - ab_cols)))], axis=1)
            proj = _norm_matmul(h, mix_norm[i], w_cat.astype(BF16), tm=tm_in, tn=LANES * 7, out_dtype=F32)
            gu_block = (A_QKV + A_V) // LANES
            gv_block = gu_block + B_W // LANES
            ab_block = gv_block + B_W // LANES
            out_a = _gdn(proj, ev_conv_w[j], ev_a_log[j], ev_dt_bias[j], ev_out_norm[j],
                         tb=_pick(t, 1024), ab_block=ab_block)
            out_b = _sgu(proj, ev_sgu_ln_g[j], ev_sgu_ln_b[j], ev_sgu_w[j], ev_sgu_b[j],
                         tb=_pick(t, 512), gu_block=gu_block, gv_block=gv_block)
            w_out = ev_w_out[j].astype(BF16)
            h = _matmul_residual([out_a, out_b], [w_out[:A_V], w_out[A_V:]], h, tm=tm, tn=1024)
        else:
            if rope is None:
                rope = _rope_tables(pos1, tm=tm)
            lam_init = 0.8 - 0.6 * math.exp(-0.3 * i)
            qkv = _norm_matmul(h, mix_norm[i], od_w_in[j].astype(BF16), tm=tm_in, tn=1024, out_dtype=BF16,
                               rope=(rope[0], rope[1], C_QK, 2 * C_QK, DIFF_DH ** -0.5))
            att = _diff_attn(qkv, od_lambda[j], od_subln[j], tq=_pick(t, 1024), lam_init=lam_init)
            h = _matmul_residual([att], [od_w_out[j].astype(BF16)], h, tm=tm, tn=1024)
        h = _moe_ple(h, p[i].reshape(t, -1), ffn_norm[i], router_w[i], router_b[i], exp_w1, exp_b1,
                     exp_w2, exp_b2, i, ple_norm[i], ple_gate[i], ple_proj[i], final_norm,
                     final=(i == depth - 1), tiles=tiles)
    return h.reshape(bsz, s, d)
```

```python
import functools
import math

import jax
import jax.numpy as jnp
from jax import lax
from jax.experimental import pallas as pl
from jax.experimental.pallas import tpu as pltpu

F32 = jnp.float32
BF16 = jnp.bfloat16
HI = lax.Precision.HIGHEST

NORM_EPS = 1e-6
L2_EPS = 1e-6
LN_EPS = 1e-5
LANES = 128
SUBLANES = 8
VMEM_LIMIT = 56 * 1024 * 1024

GDN_HEADS = 8
GDN_DK = 128
GDN_DV = 128
GDN_CONV = 4
GDN_CHUNK = 128
SGU_GROUPS = 8
SGU_DIM = 128
SGU_CHUNK = 128
DIFF_HEADS = 8
DIFF_DH = 128
ROPE_THETA = 10000.0
TOP_K = 4
SWIGLU_ALPHA = 1.702
SWIGLU_LIMIT = 7.0

A_QK = GDN_HEADS * GDN_DK
A_V = GDN_HEADS * GDN_DV
A_QKV = 2 * A_QK + A_V
B_W = SGU_GROUPS * SGU_DIM
C_QK = DIFF_HEADS * 2 * DIFF_DH


def _params(n_axes):
    return pltpu.CompilerParams(dimension_semantics=("arbitrary",) * n_axes,
                                vmem_limit_bytes=VMEM_LIMIT)


def _rms(x, g):
    return x * lax.rsqrt(jnp.mean(x * x, axis=-1, keepdims=True) + NORM_EPS) * g


def _lane_tile(x, reps):
    return jnp.concatenate([x] * reps, axis=1)


def _split_bf16(x):
    hi = x.astype(BF16)
    return hi, (x - hi.astype(F32)).astype(BF16)


def _dot3(a, b):
    ah, al = _split_bf16(a)
    bh, bl = _split_bf16(b)
    return (jnp.dot(ah, bh, preferred_element_type=F32)
            + (jnp.dot(ah, bl, preferred_element_type=F32) + jnp.dot(al, bh, preferred_element_type=F32)))


def _dot_exact_lhs(a_bf, b):
    b0 = b.astype(BF16)
    r = b - b0.astype(F32)
    b1 = r.astype(BF16)
    b2 = (r - b1.astype(F32)).astype(BF16)
    return (jnp.dot(a_bf, b0, preferred_element_type=F32)
            + (jnp.dot(a_bf, b1, preferred_element_type=F32) + jnp.dot(a_bf, b2, preferred_element_type=F32)))


def _gelu(x):
    return 0.5 * x * (1.0 + lax.erf(x * (2.0 ** -0.5)))


def _softplus(x):
    return jnp.maximum(x, 0.0) + jnp.log1p(jnp.exp(-jnp.abs(x)))


def _norm_matmul_kernel(x_ref, g_ref, w_ref, o_ref, xn_ref):
    @pl.when(pl.program_id(1) == 0)
    def _():
        xn_ref[...] = _rms(x_ref[...], g_ref[...]).astype(BF16)

    o_ref[...] = jnp.dot(xn_ref[...], w_ref[...], preferred_element_type=F32).astype(o_ref.dtype)


def _norm_matmul_rope_kernel(x_ref, g_ref, w_ref, cos_ref, sin_ref, o_ref, xn_ref, *, n_q_blocks,
                             n_rope_blocks, q_scale):
    j = pl.program_id(1)

    @pl.when(j == 0)
    def _():
        xn_ref[...] = _rms(x_ref[...], g_ref[...]).astype(BF16)

    y = jnp.dot(xn_ref[...], w_ref[...], preferred_element_type=F32)

    @pl.when(j < n_rope_blocks)
    def _():
        cos = cos_ref[...]
        sin = sin_ref[...]
        scale = jnp.where(j < n_q_blocks, q_scale, 1.0).astype(F32)
        for c in range(y.shape[1] // LANES):
            yc = y[:, c * LANES:(c + 1) * LANES]
            rot = pltpu.roll(yc, LANES // 2, axis=1)
            o_ref[:, c * LANES:(c + 1) * LANES] = ((yc * cos + rot * sin) * scale).astype(o_ref.dtype)

    @pl.when(j >= n_rope_blocks)
    def _():
        o_ref[...] = y.astype(o_ref.dtype)


def _norm_matmul(x, gain, w, *, tm, tn, out_dtype, rope=None):
    t, d = x.shape
    n = w.shape[1]
    grid = (t // tm, n // tn)
    in_specs = [pl.BlockSpec((tm, d), lambda i, j: (i, 0)),
                pl.BlockSpec((1, d), lambda i, j: (0, 0)),
                pl.BlockSpec((d, tn), lambda i, j: (0, j))]
    args = [x, gain.reshape(1, d), w]
    if rope is None:
        body = _norm_matmul_kernel
    else:
        cos, sin, n_q_cols, n_rope_cols, q_scale = rope
        body = functools.partial(_norm_matmul_rope_kernel, n_q_blocks=n_q_cols // tn,
                                 n_rope_blocks=n_rope_cols // tn, q_scale=q_scale)
        in_specs += [pl.BlockSpec((tm, LANES), lambda i, j: (i, 0)),
                     pl.BlockSpec((tm, LANES), lambda i, j: (i, 0))]
        args += [cos, sin]
    return pl.pallas_call(
        body,
        grid=grid,
        in_specs=in_specs,
        out_specs=pl.BlockSpec((tm, tn), lambda i, j: (i, j)),
        out_shape=jax.ShapeDtypeStruct((t, n), out_dtype),
        scratch_shapes=[pltpu.VMEM((tm, d), BF16)],
        compiler_params=_params(2),
    )(*args)


def _matmul_residual_kernel(*refs, n_acts):
    a_refs = refs[:n_acts]
    w_refs = refs[n_acts:2 * n_acts]
    res_ref = refs[2 * n_acts]
    o_ref = refs[2 * n_acts + 1]
    acc = res_ref[...]
    for a_ref, w_ref in zip(a_refs, w_refs):
        acc = acc + jnp.dot(a_ref[...], w_ref[...], preferred_element_type=F32)
    o_ref[...] = acc


def _matmul_residual(acts, ws, res, *, tm, tn):
    t, n = res.shape
    n_acts = len(acts)
    in_specs = [pl.BlockSpec((tm, a.shape[1]), lambda i, j: (i, 0)) for a in acts]
    in_specs += [pl.BlockSpec((w.shape[0], tn), lambda i, j: (0, j)) for w in ws]
    in_specs += [pl.BlockSpec((tm, tn), lambda i, j: (i, j))]
    return pl.pallas_call(
        functools.partial(_matmul_residual_kernel, n_acts=n_acts),
        grid=(t // tm, n // tn),
        in_specs=in_specs,
        out_specs=pl.BlockSpec((tm, tn), lambda i, j: (i, j)),
        out_shape=jax.ShapeDtypeStruct((t, n), F32),
        compiler_params=_params(2),
    )(*acts, *ws, res)


def _gdn_kernel(q_ref, k_ref, v_ref, z_ref, ab_ref, cwq_ref, cwk_ref, cwv_ref, alog_ref, dtb_ref,
                onorm_ref, o_ref, buf_ref, s_ref, *, tb):
    c_len = GDN_CHUNK
    h = pl.program_id(0)
    halo = SUBLANES

    @pl.when(pl.program_id(1) == 0)
    def _():
        buf_ref[:, 0:halo, :] = jnp.zeros((3, halo, LANES), F32)
        s_ref[...] = jnp.zeros_like(s_ref)

    def conv_silu(i, x_ref, cw_ref):
        x = x_ref[...]
        buf_ref[i, halo:halo + tb, :] = x
        cw = cw_ref[...]
        acc = x * cw[GDN_CONV - 1:GDN_CONV, :]
        for s in range(1, GDN_CONV):
            acc = acc + buf_ref[i, halo - s:halo - s + tb, :] * cw[GDN_CONV - 1 - s:GDN_CONV - s, :]
        buf_ref[i, 0:halo, :] = x[tb - halo:tb, :]
        return acc * jax.nn.sigmoid(acc)

    q = conv_silu(0, q_ref, cwq_ref)
    k = conv_silu(1, k_ref, cwk_ref)
    v = conv_silu(2, v_ref, cwv_ref)
    q = q * lax.rsqrt(jnp.sum(q * q, axis=-1, keepdims=True) + L2_EPS) * (GDN_DK ** -0.5)
    k = k * lax.rsqrt(jnp.sum(k * k, axis=-1, keepdims=True) + L2_EPS)

    ab = ab_ref[...]
    lane = lax.broadcasted_iota(jnp.int32, (1, LANES), 1)
    g_all = -jnp.exp(alog_ref[...]) * _softplus(ab + dtb_ref[...])
    g_col = jnp.sum(jnp.where(lane == h, g_all, 0.0), axis=1, keepdims=True)
    beta = jnp.sum(jnp.where(lane == GDN_HEADS + h, jax.nn.sigmoid(ab), 0.0), axis=1, keepdims=True)

    row = lax.broadcasted_iota(jnp.int32, (c_len, c_len), 0)
    col = lax.broadcasted_iota(jnp.int32, (c_len, c_len), 1)
    tri_incl_bf = (row >= col).astype(BF16)
    eye = (row == col).astype(F32)
    onorm = onorm_ref[...]
    nt = (((1,), (1,)), ((), ()))

    n_chunks = tb // c_len
    slices = [slice(c * c_len, (c + 1) * c_len) for c in range(n_chunks)]
    gbs = [_dot_exact_lhs(tri_incl_bf, jnp.broadcast_to(g_col[sl], (c_len, LANES))) for sl in slices]
    gams = [jnp.where(row >= col, jnp.exp(jnp.minimum(gb - gb.T, 0.0)), 0.0) for gb in gbs]
    kbs = [k[sl] * beta[sl] for sl in slices]
    scs = [lax.dot_general(jnp.concatenate([kb, q[sl]], axis=0).astype(BF16), k[sl].astype(BF16), nt,
                           preferred_element_type=F32) for kb, sl in zip(kbs, slices)]
    lows = [jnp.where(row > col, sc[:c_len] * gam, 0.0) for sc, gam in zip(scs, gams)]
    aqks = [(sc[c_len:] * gam).astype(BF16) for sc, gam in zip(scs, gams)]
    invs = [eye - low for low in lows]
    pws = lows
    for _ in range(int(math.log2(c_len)) - 1):
        pws = [_dot3(pw, pw) for pw in pws]
        invs = [inv + _dot3(inv, pw) for inv, pw in zip(invs, pws)]
    egs = [jnp.exp(gb) for gb in gbs]
    sols = [_dot3(inv, jnp.concatenate([v[sl] * beta[sl], kb * eg], axis=1))
            for inv, sl, kb, eg in zip(invs, slices, kbs, egs)]
    g_lasts = [gb[c_len - 1:c_len, :] for gb in gbs]
    wqs = [jnp.concatenate([sol[:, GDN_DV:], q[sl] * eg], axis=0).astype(BF16)
           for sol, sl, eg in zip(sols, slices, egs)]
    kdts = [(k[sl] * jnp.exp(g_last - gb)).T.astype(BF16) for sl, g_last, gb in zip(slices, g_lasts, gbs)]

    state = s_ref[...]
    for c, sl in enumerate(slices):
        ws = jnp.dot(wqs[c], state.astype(BF16), preferred_element_type=F32)
        v_new_bf = (sols[c][:, :GDN_DV] - ws[:c_len]).astype(BF16)
        o = ws[c_len:] + jnp.dot(aqks[c], v_new_bf, preferred_element_type=F32)
        state = state * jnp.exp(g_lasts[c]) + jnp.dot(kdts[c], v_new_bf, preferred_element_type=F32)
        zc = z_ref[sl, :]
        o = _rms(o, onorm) * (zc * jax.nn.sigmoid(zc))
        o_ref[sl, :] = o.astype(o_ref.dtype)
    s_ref[...] = state


def _gdn(proj, conv_w, a_log, dt_bias, out_norm, *, tb, ab_block):
    t = proj.shape[0]
    hds = GDN_HEADS

    def col(off):
        return pl.BlockSpec((tb, LANES), lambda h, b, off=off: (b, off + h))

    def cw(off):
        return pl.BlockSpec((GDN_CONV, LANES), lambda h, b, off=off: (0, off + h))

    vec = pl.BlockSpec((1, LANES), lambda h, b: (0, 0))
    pad = LANES - hds
    return pl.pallas_call(
        functools.partial(_gdn_kernel, tb=tb),
        grid=(hds, t // tb),
        in_specs=[col(0), col(hds), col(2 * hds), col(3 * hds),
                  pl.BlockSpec((tb, LANES), lambda h, b: (b, ab_block)),
                  cw(0), cw(hds), cw(2 * hds), vec, vec, vec],
        out_specs=pl.BlockSpec((tb, LANES), lambda h, b: (b, h)),
        out_shape=jax.ShapeDtypeStruct((t, A_V), BF16),
        scratch_shapes=[pltpu.VMEM((3, SUBLANES + tb, LANES), F32),
                        pltpu.VMEM((GDN_DK, GDN_DV), F32)],
        compiler_params=_params(2),
    )(proj, proj, proj, proj, proj, conv_w, conv_w, conv_w,
      jnp.pad(a_log, (0, pad)).reshape(1, LANES), jnp.pad(dt_bias, (0, pad)).reshape(1, LANES),
      out_norm.reshape(1, LANES))


def _sgu_kernel(gu_ref, gv_ref, lng_ref, lnb_ref, w_ref, b_ref, o_ref, *, tb):
    c_len = SGU_CHUNK
    row = lax.broadcasted_iota(jnp.int32, (c_len, c_len), 0)
    col = lax.broadcasted_iota(jnp.int32, (c_len, c_len), 1)
    w = jnp.where(row >= col, w_ref[...], 0.0)
    bias = b_ref[...]
    gu = _gelu(gu_ref[...])
    gv = _gelu(gv_ref[...])
    mu = jnp.mean(gv, axis=-1, keepdims=True)
    xc = gv - mu
    var = jnp.mean(xc * xc, axis=-1, keepdims=True)
    gv = xc * lax.rsqrt(var + LN_EPS) * lng_ref[...] + lnb_ref[...]
    for c in range(tb // c_len):
        sl = slice(c * c_len, (c + 1) * c_len)
        mixed = jnp.dot(w, gv[sl], precision=HI, preferred_element_type=F32) + bias
        o_ref[sl, :] = (gu[sl] * mixed).astype(o_ref.dtype)


def _sgu(proj, ln_g, ln_b, sgu_w, sgu_b, *, tb, gu_block, gv_block):
    t = proj.shape[0]
    g = SGU_GROUPS
    return pl.pallas_call(
        functools.partial(_sgu_kernel, tb=tb),
        grid=(t // tb, g),
        in_specs=[pl.BlockSpec((tb, LANES), lambda b, gi: (b, gu_block + gi)),
                  pl.BlockSpec((tb, LANES), lambda b, gi: (b, gv_block + gi)),
                  pl.BlockSpec((None, 1, SGU_DIM), lambda b, gi: (gi, 0, 0)),
                  pl.BlockSpec((None, 1, SGU_DIM), lambda b, gi: (gi, 0, 0)),
                  pl.BlockSpec((None, SGU_CHUNK, SGU_CHUNK), lambda b, gi: (gi, 0, 0)),
                  pl.BlockSpec((None, SGU_CHUNK, 1), lambda b, gi: (gi, 0, 0))],
        out_specs=pl.BlockSpec((tb, LANES), lambda b, gi: (b, gi)),
        out_shape=jax.ShapeDtypeStruct((t, B_W), BF16),
        compiler_params=_params(2),
    )(proj, proj, ln_g.reshape(g, 1, SGU_DIM), ln_b.reshape(g, 1, SGU_DIM), sgu_w,
      sgu_b.reshape(g, SGU_CHUNK, 1))


def _rope_table_kernel(pos_ref, freq_ref, cos_ref, sin_ref):
    ang = pos_ref[...].astype(F32) * freq_ref[...]
    lane = lax.broadcasted_iota(jnp.int32, ang.shape, 1)
    cos_ref[...] = jnp.cos(ang)
    sin_ref[...] = jnp.where(lane < LANES // 2, -1.0, 1.0) * jnp.sin(ang)


def _rope_tables(positions, *, tm):
    t = positions.shape[0]
    half = DIFF_DH // 2
    inv_freq = ROPE_THETA ** (-jnp.arange(half, dtype=F32) / half)
    freq = jnp.concatenate([inv_freq, inv_freq]).reshape(1, LANES)
    return pl.pallas_call(
        _rope_table_kernel,
        grid=(t // tm,),
        in_specs=[pl.BlockSpec((tm, 1), lambda i: (i, 0)),
                  pl.BlockSpec((1, LANES), lambda i: (0, 0))],
        out_specs=[pl.BlockSpec((tm, LANES), lambda i: (i, 0)),
                   pl.BlockSpec((tm, LANES), lambda i: (i, 0))],
        out_shape=[jax.ShapeDtypeStruct((t, LANES), F32), jax.ShapeDtypeStruct((t, LANES), F32)],
        compiler_params=_params(1),
    )(positions.reshape(t, 1), freq)


def _diff_attn_kernel(qi_ref, ki_ref, q_ref, k_ref, v_ref, lam_ref, subln_ref, o_ref, m_ref, l_ref,
                      acc_ref, *, lam_init, row_split):
    p = pl.program_id(1)
    qi = qi_ref[p]
    ki = ki_ref[p]
    dh = DIFF_DH
    nt = (((1,), (1,)), ((), ()))

    @pl.when(ki == 0)
    def _():
        m_ref[...] = jnp.full(m_ref.shape, -jnp.inf, F32)
        l_ref[...] = jnp.zeros_like(l_ref)
        acc_ref[...] = jnp.zeros_like(acc_ref)

    def step(masked):
        v = v_ref[...]
        tq = q_ref.shape[0]
        rows = tq // row_split
        chains = [(mm, r * rows) for mm in range(2) for r in range(row_split)]
        ss = [lax.dot_general(q_ref[r0:r0 + rows, mm * dh:(mm + 1) * dh], k_ref[:, mm * dh:(mm + 1) * dh],
                              nt, preferred_element_type=F32) for mm, r0 in chains]
        if masked:
            col = lax.broadcasted_iota(jnp.int32, ss[0].shape, 1)
            row = lax.broadcasted_iota(jnp.int32, ss[0].shape, 0)
            ss = [jnp.where(row + r0 >= col, s, -jnp.inf) for s, (mm, r0) in zip(ss, chains)]
        m_prevs = [m_ref[mm, r0:r0 + rows, :] for mm, r0 in chains]
        m_news = [jnp.maximum(m_prev, jnp.max(s, axis=1, keepdims=True)) for m_prev, s in zip(m_prevs, ss)]
        alphas = [jnp.exp(m_prev - m_new) for m_prev, m_new in zip(m_prevs, m_news)]
        prs = [jnp.exp(s - _lane_tile(m_new, s.shape[1] // LANES)) for s, m_new in zip(ss, m_news)]
        for (mm, r0), alpha, pr, m_new in zip(chains, alphas, prs, m_news):
            l_ref[mm, r0:r0 + rows, :] = alpha * l_ref[mm, r0:r0 + rows, :] + jnp.sum(pr, axis=1, keepdims=True)
            m_ref[mm, r0:r0 + rows, :] = m_new
        pvs = [jnp.dot(pr.astype(BF16), v, preferred_element_type=F32) for pr in prs]
        for (mm, r0), alpha, pv in zip(chains, alphas, pvs):
            acc_ref[mm, r0:r0 + rows, :] = (_lane_tile(alpha, acc_ref.shape[2] // LANES)
                                            * acc_ref[mm, r0:r0 + rows, :] + pv)

    @pl.when(ki < qi)
    def _():
        step(False)

    @pl.when(ki == qi)
    def _():
        step(True)
        lf = lam_ref[...]
        d01 = jnp.sum(lf[0:1] * lf[1:2], axis=1, keepdims=True)
        d23 = jnp.sum(lf[2:3] * lf[3:4], axis=1, keepdims=True)
        lam = jnp.exp(d01) - jnp.exp(d23) + lam_init
        reps = acc_ref.shape[2] // LANES
        o = (acc_ref[0] / _lane_tile(l_ref[0], reps)
             - lam * (acc_ref[1] / _lane_tile(l_ref[1], reps)))
        o = _rms(o, subln_ref[...]) * (1.0 - lam_init)
        o_ref[...] = o.astype(o_ref.dtype)


def _diff_attn(qkv, lam, subln, *, tq, lam_init, row_split=2):
    t = qkv.shape[0]
    nq = t // tq
    hw = 2 * DIFF_DH
    qi_tab = jnp.asarray([qi for qi in range(nq) for _ in range(qi + 1)], jnp.int32)
    ki_tab = jnp.asarray([ki for qi in range(nq) for ki in range(qi + 1)], jnp.int32)
    k_off = C_QK // hw
    grid_spec = pltpu.PrefetchScalarGridSpec(
        num_scalar_prefetch=2,
        grid=(DIFF_HEADS, qi_tab.shape[0]),
        in_specs=[pl.BlockSpec((tq, hw), lambda h, p, qt, kt: (qt[p], h)),
                  pl.BlockSpec((tq, hw), lambda h, p, qt, kt: (kt[p], k_off + h)),
                  pl.BlockSpec((tq, hw), lambda h, p, qt, kt: (kt[p], 2 * k_off + h)),
                  pl.BlockSpec((4, DIFF_DH), lambda h, p, qt, kt: (0, 0)),
                  pl.BlockSpec((1, hw), lambda h, p, qt, kt: (0, 0))],
        out_specs=pl.BlockSpec((tq, hw), lambda h, p, qt, kt: (qt[p], h)),
        scratch_shapes=[pltpu.VMEM((2, tq, LANES), F32), pltpu.VMEM((2, tq, LANES), F32),
                        pltpu.VMEM((2, tq, hw), F32)])
    return pl.pallas_call(
        functools.partial(_diff_attn_kernel, lam_init=lam_init, row_split=row_split),
        grid_spec=grid_spec,
        out_shape=jax.ShapeDtypeStruct((t, DIFF_HEADS * hw), BF16),
        compiler_params=_params(2),
    )(qi_tab, ki_tab, qkv, qkv, qkv, lam, subln.reshape(1, hw))


def _router_kernel(h_ref, g_ref, rw_ref, rb_ref, xp_ref, ids_ref, gates_ref, rank_ref, cnt_ref,
                   carry_ref, *, n_experts):
    tm = h_ref.shape[0]
    half = h_ref.shape[1] // 2

    @pl.when(pl.program_id(0) == 0)
    def _():
        carry_ref[...] = jnp.zeros_like(carry_ref)

    xn = _rms(h_ref[...], g_ref[...])
    bits = lax.bitcast_convert_type(xn.astype(BF16).astype(F32), jnp.uint32)
    xp_ref[...] = (bits[:, :half] >> 16) | bits[:, half:]

    logits = jnp.dot(xn, rw_ref[...], precision=HI, preferred_element_type=F32) + rb_ref[...]
    lane = lax.broadcasted_iota(jnp.int32, (tm, LANES), 1)
    lane_f = lane.astype(F32)
    cur = jnp.where(lane < n_experts, logits, -jnp.inf)
    ids, vals = [], []
    for _ in range(TOP_K):
        m = jnp.max(cur, axis=1, keepdims=True)
        idx = jnp.min(jnp.where(cur == m, lane_f, float(LANES)), axis=1, keepdims=True)
        ids.append(idx)
        vals.append(m)
        cur = jnp.where(lane_f == idx, -jnp.inf, cur)
    exps = [jnp.exp(v - vals[0]) for v in vals]
    denom = exps[0] + exps[1] + exps[2] + exps[3]
    hot = [(lane_f == idx) for idx in ids]
    multi = jnp.zeros((tm, LANES), F32)
    for hk in hot:
        multi = multi + hk.astype(F32)
    row = lax.broadcasted_iota(jnp.int32, (tm, tm), 0)
    col = lax.broadcasted_iota(jnp.int32, (tm, tm), 1)
    strict = (row > col).astype(BF16)
    before = jnp.dot(strict, multi.astype(BF16), preferred_element_type=F32) + carry_ref[...]
    ids_out = jnp.zeros((tm, LANES), F32)
    gates_out = jnp.zeros((tm, LANES), F32)
    rank_out = jnp.zeros((tm, LANES), F32)
    for kk in range(TOP_K):
        rank = jnp.sum(jnp.where(hot[kk], before, 0.0), axis=1, keepdims=True)
        ids_out = jnp.where(lane == kk, ids[kk], ids_out)
        gates_out = jnp.where(lane == kk, exps[kk] / denom, gates_out)
        rank_out = jnp.where(lane == kk, rank, rank_out)
    ids_ref[...] = ids_out.astype(jnp.int32)
    gates_ref[...] = gates_out
    rank_ref[...] = rank_out.astype(jnp.int32)
    carry_ref[...] = carry_ref[...] + jnp.sum(multi, axis=0, keepdims=True)
    cnt_ref[...] = carry_ref[...]


def _router(h, gain, router_w, router_b, *, tm):
    t, d = h.shape
    e = router_w.shape[1]
    rw = jnp.pad(router_w, ((0, 0), (0, LANES - e)))
    rb = jnp.pad(router_b, (0, LANES - e)).reshape(1, LANES)
    row_spec = pl.BlockSpec((tm, LANES), lambda i: (i, 0))
    return pl.pallas_call(
        functools.partial(_router_kernel, n_experts=e),
        grid=(t // tm,),
        in_specs=[pl.BlockSpec((tm, d), lambda i: (i, 0)),
                  pl.BlockSpec((1, d), lambda i: (0, 0)),
                  pl.BlockSpec((d, LANES), lambda i: (0, 0)),
                  pl.BlockSpec((1, LANES), lambda i: (0, 0))],
        out_specs=[pl.BlockSpec((tm, d // 2), lambda i: (i, 0)), row_spec, row_spec, row_spec,
                   pl.BlockSpec((1, LANES), lambda i: (0, 0))],
        out_shape=[jax.ShapeDtypeStruct((t, d // 2), jnp.uint32),
                   jax.ShapeDtypeStruct((t, LANES), jnp.int32),
                   jax.ShapeDtypeStruct((t, LANES), F32),
                   jax.ShapeDtypeStruct((t, LANES), jnp.int32),
                   jax.ShapeDtypeStruct((1, LANES), F32)],
        scratch_shapes=[pltpu.VMEM((1, LANES), F32)],
        compiler_params=_params(1),
    )(h, gain.reshape(1, d), rw, rb)


def _row_copy(src_ref, dst_ref, sem, src_row, dst_row):
    return pltpu.make_async_copy(src_ref.at[pl.ds(src_row, 1)], dst_ref.at[pl.ds(dst_row, 1)], sem)


DMA_UNROLL = 8


def _issue_row_gather(src_ref, dst_ref, sem, index_of, rows, skip_negative=False):
    def body(j0, carry):
        for u in range(DMA_UNROLL):
            j = j0 * DMA_UNROLL + u
            idx = index_of(j)
            if skip_negative:
                @pl.when(idx >= 0)
                def _():
                    _row_copy(src_ref, dst_ref, sem, idx, j).start(priority=u % 2)
            else:
                _row_copy(src_ref, dst_ref, sem, idx, j).start(priority=u % 2)
        return carry

    lax.fori_loop(0, rows // DMA_UNROLL, body, 0)


def _wait_row_gather(src_ref, dst_ref, sem, rows, index_of=None):
    def body(j, carry):
        if index_of is None:
            _row_copy(src_ref, dst_ref, sem, 0, j).wait()
        else:
            @pl.when(index_of(j) >= 0)
            def _():
                _row_copy(src_ref, dst_ref, sem, 0, j).wait()
        return carry

    lax.fori_loop(0, rows, body, 0, unroll=DMA_UNROLL)


def _gather_rows_kernel(idx_ref, nxt_ref, src_ref, o_ref, buf_ref, sem, *, rows):
    i = pl.program_id(0)
    slot = i % 2

    @pl.when(i == 0)
    def _():
        buf_ref[...] = jnp.zeros_like(buf_ref)
        _issue_row_gather(src_ref, buf_ref.at[0], sem.at[0], lambda j: idx_ref[0, 0, j], rows, True)

    @pl.when(i + 1 < pl.num_programs(0))
    def _():
        _issue_row_gather(src_ref, buf_ref.at[1 - slot], sem.at[1 - slot], lambda j: nxt_ref[0, 0, j], rows,
                          True)

    _wait_row_gather(src_ref, buf_ref.at[slot], sem.at[slot], rows, lambda j: idx_ref[0, 0, j])
    o_ref[...] = buf_ref[slot]


def _gather_rows(src, idx, *, rows):
    n = idx.shape[0]
    width = src.shape[1]
    steps = n // rows
    idx3 = idx.reshape(steps, 1, rows)
    return pl.pallas_call(
        functools.partial(_gather_rows_kernel, rows=rows),
        grid=(steps,),
        in_specs=[pl.BlockSpec((1, 1, rows), lambda i: (i, 0, 0), memory_space=pltpu.SMEM),
                  pl.BlockSpec((1, 1, rows), lambda i: (jnp.minimum(i + 1, steps - 1), 0, 0),
                               memory_space=pltpu.SMEM),
                  pl.BlockSpec(memory_space=pl.ANY)],
        out_specs=pl.BlockSpec((rows, width), lambda i: (i, 0)),
        out_shape=jax.ShapeDtypeStruct((n, width), src.dtype),
        scratch_shapes=[pltpu.VMEM((2, rows, width), src.dtype), pltpu.SemaphoreType.DMA((2,))],
        compiler_params=_params(1),
    )(idx3, idx3, src)


ITEM_SPARE, ITEM_FULL, ITEM_HALF = 0, 1, 2

def _moe_up_kernel(*refs):
    tables = refs[:_Items.COUNT]
    x_ref, wga_ref, wla_ref, wgb_ref, wlb_ref, bg_ref, bl_ref, o_ref, wg_s, wl_s = refs[_Items.COUNT:]
    first_ref, mode_ref, par_ref = tables[_Items.FIRST], tables[_Items.MODE], tables[_Items.PARITY]
    i = pl.program_id(0)
    tm = x_ref.shape[0]
    half = x_ref.shape[1]

    @pl.when((first_ref[i] == 1) & (par_ref[i] == 0))
    def _():
        wg_s[...] = wga_ref[...].astype(BF16)
        wl_s[...] = wla_ref[...].astype(BF16)

    @pl.when((first_ref[i] == 1) & (par_ref[i] == 1))
    def _():
        wg_s[...] = wgb_ref[...].astype(BF16)
        wl_s[...] = wlb_ref[...].astype(BF16)

    @pl.when(mode_ref[i] == ITEM_SPARE)
    def _():
        o_ref[...] = jnp.zeros_like(o_ref)

    def compute(n_rows):
        words = x_ref[:n_rows, :]
        lo = lax.bitcast_convert_type(words << 16, F32).astype(BF16)
        hi = lax.bitcast_convert_type(words & jnp.uint32(0xFFFF0000), F32).astype(BF16)

        def proj(w_s, b_ref):
            return (jnp.dot(lo, w_s[:half, :], preferred_element_type=F32)
                    + jnp.dot(hi, w_s[half:, :], preferred_element_type=F32) + b_ref[...])

        x_glu = jnp.minimum(proj(wg_s, bg_ref), SWIGLU_LIMIT)
        x_lin = jnp.clip(proj(wl_s, bl_ref), -SWIGLU_LIMIT, SWIGLU_LIMIT)
        act = x_glu * jax.nn.sigmoid(SWIGLU_ALPHA * x_glu) * (x_lin + 1.0)
        o_ref[:n_rows, :] = act.astype(o_ref.dtype)

    @pl.when(mode_ref[i] == ITEM_FULL)
    def _():
        compute(tm)

    @pl.when(mode_ref[i] == ITEM_HALF)
    def _():
        compute(tm // 2)
        o_ref[tm // 2:, :] = jnp.zeros((tm - tm // 2, o_ref.shape[1]), o_ref.dtype)


class _Items:
    COUNT = 12
    X_TILE, EXPERT, CHUNK, OUT_TILE, OUT_CHUNK, FIRST, MODE, EXPERT_A, CHUNK_A, EXPERT_B, CHUNK_B, PARITY = range(COUNT)


def _item_map(fn):
    return lambda i, *tables: fn(i, tables)


def _moe_up(x_sorted, w1, b1, layer, tables, *, tm, fc):
    p_rows, half = x_sorted.shape
    depth, e, d, ff2 = w1.shape
    ff = ff2 // 2
    ncb = ff // fc
    n_items = tables[0].shape[0]
    it = _Items

    def w_spec(e_tab, c_tab, off):
        return pl.BlockSpec((None, None, d, fc),
                            _item_map(lambda i, t: (layer, t[e_tab][i], 0, off + t[c_tab][i])))

    def b_spec(off):
        return pl.BlockSpec((None, None, 1, fc),
                            _item_map(lambda i, t: (layer, t[it.EXPERT][i], 0, off + t[it.CHUNK][i])))

    grid_spec = pltpu.PrefetchScalarGridSpec(
        num_scalar_prefetch=len(tables),
        grid=(n_items,),
        in_specs=[pl.BlockSpec((tm, half), _item_map(lambda i, t: (t[it.X_TILE][i], 0))),
                  w_spec(it.EXPERT_A, it.CHUNK_A, 0), w_spec(it.EXPERT_A, it.CHUNK_A, ncb),
                  w_spec(it.EXPERT_B, it.CHUNK_B, 0), w_spec(it.EXPERT_B, it.CHUNK_B, ncb),
                  b_spec(0), b_spec(ncb)],
        out_specs=pl.BlockSpec((tm, fc), _item_map(lambda i, t: (t[it.OUT_TILE][i], t[it.OUT_CHUNK][i]))),
        scratch_shapes=[pltpu.VMEM((d, fc), BF16), pltpu.VMEM((d, fc), BF16)])
    b1r = b1.reshape(depth, e, 1, ff2)
    return pl.pallas_call(
        _moe_up_kernel,
        grid_spec=grid_spec,
        out_shape=jax.ShapeDtypeStruct((p_rows, ff), BF16),
        compiler_params=_params(1),
    )(*tables, x_sorted, w1, w1, w1, w1, b1r, b1r)


def _moe_down_kernel(*refs):
    tables = refs[:_Items.COUNT]
    a_ref, wa_ref, wb_ref, b_ref, o_ref, w_s = refs[_Items.COUNT:]
    first_ref, mode_ref, par_ref = tables[_Items.FIRST], tables[_Items.MODE], tables[_Items.PARITY]
    i = pl.program_id(0)
    tm = a_ref.shape[0]

    @pl.when((first_ref[i] == 1) & (par_ref[i] == 0))
    def _():
        w_s[...] = wa_ref[...].astype(BF16)

    @pl.when((first_ref[i] == 1) & (par_ref[i] == 1))
    def _():
        w_s[...] = wb_ref[...].astype(BF16)

    @pl.when(mode_ref[i] == ITEM_SPARE)
    def _():
        o_ref[...] = jnp.zeros_like(o_ref)

    @pl.when(mode_ref[i] == ITEM_FULL)
    def _():
        o_ref[...] = jnp.dot(a_ref[...], w_s[...], preferred_element_type=F32) + b_ref[...]

    @pl.when(mode_ref[i] == ITEM_HALF)
    def _():
        o_ref[:tm // 2, :] = (jnp.dot(a_ref[:tm // 2, :], w_s[...], preferred_element_type=F32)
                              + b_ref[...])
        o_ref[tm // 2:, :] = jnp.zeros((tm - tm // 2, o_ref.shape[1]), o_ref.dtype)


def _moe_down(act, w2, b2, layer, tables, *, tm, nc):
    p_rows, ff = act.shape
    depth, e, _, d = w2.shape
    n_items = tables[0].shape[0]
    it = _Items

    def w_spec(e_tab, c_tab):
        return pl.BlockSpec((None, None, ff, nc), _item_map(lambda i, t: (layer, t[e_tab][i], 0, t[c_tab][i])))

    grid_spec = pltpu.PrefetchScalarGridSpec(
        num_scalar_prefetch=len(tables),
        grid=(n_items,),
        in_specs=[pl.BlockSpec((tm, ff), _item_map(lambda i, t: (t[it.X_TILE][i], 0))),
                  w_spec(it.EXPERT_A, it.CHUNK_A), w_spec(it.EXPERT_B, it.CHUNK_B),
                  pl.BlockSpec((None, None, 1, nc),
                               _item_map(lambda i, t: (layer, t[it.EXPERT][i], 0, t[it.CHUNK][i])))],
        out_specs=pl.BlockSpec((tm, nc), _item_map(lambda i, t: (t[it.OUT_TILE][i], t[it.OUT_CHUNK][i]))),
        scratch_shapes=[pltpu.VMEM((ff, nc), BF16)])
    return pl.pallas_call(
        _moe_down_kernel,
        grid_spec=grid_spec,
        out_shape=jax.ShapeDtypeStruct((p_rows, d), F32),
        compiler_params=_params(1),
    )(*tables, act, w2, w2, b2.reshape(depth, e, 1, d))


def _item_tables(counts, tm_e, tile_start_e, n_tiles_max, n_chunks):
    tiles_per_e = (counts + tm_e - 1) // tm_e
    n_items = n_tiles_max * n_chunks
    items_per_e = tiles_per_e * n_chunks
    item_end = jnp.cumsum(items_per_e)
    item_start = item_end - items_per_e
    n_valid = item_end[-1]
    n_used_tiles = n_valid // n_chunks
    i = jnp.arange(n_items, dtype=jnp.int32)
    valid = i < n_valid
    ic = jnp.minimum(i, n_valid - 1)
    ex = jnp.sum((item_end[None, :] <= ic[:, None]).astype(jnp.int32), axis=1)
    ex = jnp.minimum(ex, tiles_per_e.shape[0] - 1)
    local = ic - item_start[ex]
    per = jnp.maximum(tiles_per_e[ex], 1)
    ch = local // per
    r_local = local % per
    tile = tile_start_e[ex] + r_local
    spare = i - n_valid
    out_tile = jnp.where(valid, tile, n_used_tiles + spare // n_chunks)
    out_chunk = jnp.where(valid, ch, spare % n_chunks)
    first = valid & (r_local == 0)
    last_rows = counts[ex] - (per - 1) * tm_e
    is_half = (r_local == per - 1) & (last_rows <= tm_e // 2)
    mode = jnp.where(valid, jnp.where(is_half, ITEM_HALF, ITEM_FULL), ITEM_SPARE)
    parity = (jnp.cumsum(first.astype(jnp.int32)) - 1) % 2
    next_first = lax.cummin(jnp.where(first, i, n_items), axis=0, reverse=True)
    next_first = jnp.concatenate([next_first[1:], jnp.full((1,), n_items, jnp.int32)])
    has_next = next_first < n_items
    nxt = jnp.minimum(next_first, n_items - 1)
    ex_next = jnp.where(has_next, ex[nxt], ex)
    ch_next = jnp.where(has_next, ch[nxt], ch)
    ex_a = jnp.where(parity == 0, ex, ex_next)
    ch_a = jnp.where(parity == 0, ch, ch_next)
    ex_b = jnp.where(parity == 1, ex, ex_next)
    ch_b = jnp.where(parity == 1, ch, ch_next)
    return tuple(a.astype(jnp.int32) for a in (tile, ex, ch, out_tile, out_chunk, first, mode,
                                               ex_a, ch_a, ex_b, ch_b, parity))


def _post_kernel(pos_ref, nxt_ref, h_ref, gates_ref, y_ref, p_ref, pn_ref, pg_ref, pp_ref, fn_ref, o_ref,
                 ybuf, sem, *, final):
    tm = h_ref.shape[0]
    rows = TOP_K * tm
    i = pl.program_id(0)
    slot = i % 2

    @pl.when(i == 0)
    def _():
        _issue_row_gather(y_ref, ybuf.at[0], sem.at[0], lambda j: pos_ref[0, 0, j], rows)

    @pl.when(i + 1 < pl.num_programs(0))
    def _():
        _issue_row_gather(y_ref, ybuf.at[1 - slot], sem.at[1 - slot], lambda j: nxt_ref[0, 0, j], rows)

    _wait_row_gather(y_ref, ybuf.at[slot], sem.at[slot], rows)

    gates = gates_ref[...]
    h2 = h_ref[...]
    for k in range(TOP_K):
        h2 = h2 + gates[:, k:k + 1] * ybuf[slot, k * tm:(k + 1) * tm, :]
    hn = _rms(h2, pn_ref[...]).astype(BF16)
    gate = jax.nn.sigmoid(jnp.dot(hn, pg_ref[...], preferred_element_type=F32))
    pe = jnp.dot(p_ref[...].astype(BF16), pp_ref[...], preferred_element_type=F32)
    h3 = h2 + pe * gate
    if final:
        h3 = _rms(h3, fn_ref[...])
    o_ref[...] = h3


def _post(h, pos, gates, y, p, ple_norm, ple_gate, ple_proj, final_norm, *, tm, final):
    t, d = h.shape
    pd = p.shape[1]
    const = lambda i: (0, 0)
    steps = t // tm
    pos3 = pos.reshape(steps, tm, TOP_K).transpose(0, 2, 1).reshape(steps, 1, tm * TOP_K)
    return pl.pallas_call(
        functools.partial(_post_kernel, final=final),
        grid=(steps,),
        in_specs=[pl.BlockSpec((1, 1, tm * TOP_K), lambda i: (i, 0, 0), memory_space=pltpu.SMEM),
                  pl.BlockSpec((1, 1, tm * TOP_K), lambda i: (jnp.minimum(i + 1, steps - 1), 0, 0),
                               memory_space=pltpu.SMEM),
                  pl.BlockSpec((tm, d), lambda i: (i, 0)),
                  pl.BlockSpec((tm, LANES), lambda i: (i, 0)),
                  pl.BlockSpec(memory_space=pl.ANY),
                  pl.BlockSpec((tm, pd), lambda i: (i, 0)),
                  pl.BlockSpec((1, d), const),
                  pl.BlockSpec((d, d), const),
                  pl.BlockSpec((pd, d), const),
                  pl.BlockSpec((1, d), const)],
        out_specs=pl.BlockSpec((tm, d), lambda i: (i, 0)),
        out_shape=jax.ShapeDtypeStruct((t, d), F32),
        scratch_shapes=[pltpu.VMEM((2, TOP_K * tm, d), F32), pltpu.SemaphoreType.DMA((2,))],
        compiler_params=_params(1),
    )(pos3, pos3, h, gates, y, p, ple_norm.reshape(1, d), ple_gate, ple_proj, final_norm.reshape(1, d))


def _moe_ple(h, p, ffn_norm, router_w, router_b, w1, b1, w2, b2, layer, ple_norm, ple_gate, ple_proj,
             final_norm, *, final, tiles):
    t, d = h.shape
    e = router_w.shape[1]
    tm_r, tm_e, fc, nc, tm_p, g_rows = tiles
    xp, ids, gates, rank, cnt = _router(h, ffn_norm, router_w, router_b, tm=tm_r)

    counts = cnt[0, :e].astype(jnp.int32)
    tiles_per_e = (counts + tm_e - 1) // tm_e
    padded = tiles_per_e * tm_e
    ends = jnp.cumsum(padded)
    starts = ends - padded
    ids4 = ids[:, :TOP_K]
    start_of = jnp.sum(jnp.where(ids4[:, :, None] == jnp.arange(e, dtype=jnp.int32), starts, 0), axis=-1)
    pos = start_of + rank[:, :TOP_K]
    n_tiles_max = (t * TOP_K) // tm_e + e
    p_rows = n_tiles_max * tm_e
    tok = jnp.broadcast_to(jnp.arange(t, dtype=jnp.int32)[:, None], (t, TOP_K))
    sorted_tok = jnp.full((p_rows,), -1, jnp.int32).at[pos.reshape(-1)].set(tok.reshape(-1))
    tile_start_e = starts // tm_e

    x_sorted = _gather_rows(xp, sorted_tok, rows=g_rows)
    ff = w1.shape[3] // 2
    act = _moe_up(x_sorted, w1, b1, layer, _item_tables(counts, tm_e, tile_start_e, n_tiles_max, ff // fc),
                  tm=tm_e, fc=fc)
    y = _moe_down(act, w2, b2, layer, _item_tables(counts, tm_e, tile_start_e, n_tiles_max, d // nc),
                  tm=tm_e, nc=nc)
    return _post(h, pos, gates, y, p, ple_norm, ple_gate.astype(BF16), ple_proj.astype(BF16),
                 final_norm, tm=tm_p, final=final)


def _pick(t, pref):
    return min(pref, t)


def kernel(x, p, positions, mix_norm, ffn_norm, ev_w_in, ev_conv_w, ev_a_log, ev_dt_bias, ev_out_norm, ev_sgu_ln_g, ev_sgu_ln_b, ev_sgu_w, ev_sgu_b, ev_w_out, od_w_in, od_lambda, od_subln, od_w_out, router_w, router_b, exp_w1, exp_b1, exp_w2, exp_b2, ple_proj, ple_norm, ple_gate, final_norm):
    bsz, s, d = x.shape
    depth = mix_norm.shape[0]
    t = bsz * s
    assert bsz == 1, "sequence mixers here assume one sequence"
    h = x.reshape(t, d)
    pos1 = positions.reshape(t)

    tm = _pick(t, 512)
    tm_in = _pick(t, 1024)
    tiles = (_pick(t, 512), 512, 512, 1024, _pick(t, 256), _pick(t, 512))
    rope = None

    for i in range(depth):
        j = i // 2
        if i % 2 == 0:
            w_in = ev_w_in[j]
            ab_cols = 2 * GDN_HEADS
            g_off = A_QKV + A_V + ab_cols
            w_cat = jnp.concatenate(
                [w_in[:, :A_QKV + A_V], w_in[:, g_off:],
                 jnp.pad(w_in[:, A_QKV + A_V:g_off], ((0, 0), (0, LANES - ab_cols)))], axis=1)
            proj = _norm_matmul(h, mix_norm[i], w_cat.astype(BF16), tm=tm_in, tn=LANES * 7, out_dtype=F32)
            gu_block = (A_QKV + A_V) // LANES
            gv_block = gu_block + B_W // LANES
            ab_block = gv_block + B_W // LANES
            out_a = _gdn(proj, ev_conv_w[j], ev_a_log[j], ev_dt_bias[j], ev_out_norm[j],
                         tb=_pick(t, 1024), ab_block=ab_block)
            out_b = _sgu(proj, ev_sgu_ln_g[j], ev_sgu_ln_b[j], ev_sgu_w[j], ev_sgu_b[j],
                         tb=_pick(t, 512), gu_block=gu_block, gv_block=gv_block)
            w_out = ev_w_out[j].astype(BF16)
            h = _matmul_residual([out_a, out_b], [w_out[:A_V], w_out[A_V:]], h, tm=tm, tn=1024)
        else:
            if rope is None:
                rope = _rope_tables(pos1, tm=tm)
            lam_init = 0.8 - 0.6 * math.exp(-0.3 * i)
            qkv = _norm_matmul(h, mix_norm[i], od_w_in[j].astype(BF16), tm=tm_in, tn=1024, out_dtype=BF16,
                               rope=(rope[0], rope[1], C_QK, 2 * C_QK, DIFF_DH ** -0.5))
            att = _diff_attn(qkv, od_lambda[j], od_subln[j], tq=_pick(t, 1024), lam_init=lam_init)
            h = _matmul_residual([att], [od_w_out[j].astype(BF16)], h, tm=tm, tn=1024)
        h = _moe_ple(h, p[i].reshape(t, -1), ffn_norm[i], router_w[i], router_b[i], exp_w1, exp_b1,
                     exp_w2, exp_b2, i, ple_norm[i], ple_gate[i], ple_proj[i], final_norm,
                     final=(i == depth - 1), tiles=tiles)
    return h.reshape(bsz, s, d)
```
